```python
import jax, jax.numpy as jnp
from jax import lax
import numpy as np

D_MODEL = 1024
BATCH = 4
SEQ = 4096
DEPTH = 1
DEC_BATCH = 128
DEC_SEQ = 8
PAST_LEN = 8192
PAGE_SIZE = 128

RET_HEADS = 4
RET_DK = 128
RET_DV = 256
RET_QK_W = RET_HEADS * RET_DK
RET_V_W = RET_HEADS * RET_DV
RET_CHUNK = 128
ATT_GROUPS = ((128, 1), (512, 4), (2048, 16))
ATT_HPG = 4
ATT_HEADS = ATT_HPG * len(ATT_GROUPS)
ATT_DH = 64
ATT_W = ATT_HEADS * ATT_DH
ATT_OUT_W = ATT_HPG * ATT_DH
BAND_BLOCK = 128
IN_SPLITS = (RET_QK_W, RET_QK_W, RET_V_W, RET_V_W, ATT_W, ATT_W, ATT_W, D_MODEL, D_MODEL)
N_IN = sum(IN_SPLITS)
N_EXPERTS = 32
TOP_K = 4
D_EXPERT = 1024
SWIGLU_LIMIT = 7.0
SWIGLU_ALPHA = 1.702
PLE_DIM = 256
EPS = 1e-6
NEG_INF = -1e30

kernel_name = 'hybrid_retention_dilated_attn_moe_step'


def rmsnorm(x, g):
    xf = x.astype(jnp.float32)
    y = xf * lax.rsqrt(jnp.mean(xf * xf, axis=-1, keepdims=True) + EPS)
    return (y * g.astype(jnp.float32)).astype(x.dtype)


def split_in(z):
    offs, acc = [], 0
    for w in IN_SPLITS[:-1]:
        acc += w
        offs.append(acc)
    return jnp.split(z, offs, axis=-1)


def alibi_slopes():
    return jnp.exp2(-8.0 * (jnp.arange(ATT_HEADS, dtype=jnp.float32) + 1.0) / ATT_HEADS)


def retention_log_decay():
    return jnp.log1p(-jnp.exp2(-5.0 - jnp.arange(RET_HEADS, dtype=jnp.float32)))


def masked_softmax_stats(s, valid):
    s = jnp.where(valid, s, NEG_INF)
    m = jnp.max(s, axis=-1, keepdims=True)
    e = jnp.exp(s - m)
    den = jnp.sum(e, axis=-1, keepdims=True)
    return e / den, (m + jnp.log(den))[..., 0]


def retention_scan(q, k, v, s0, chunk):
    B, T = q.shape[:2]
    n = T // chunk
    lg = retention_log_decay()
    pos = jnp.arange(chunk, dtype=jnp.float32)
    rel = pos[:, None] - pos[None, :]
    intra = jnp.where(rel[None] >= 0, jnp.exp(lg[:, None, None] * jnp.maximum(rel, 0.0)[None]), 0.0)
    q_dec = jnp.exp(lg[None, :] * (pos[:, None] + 1.0))[None, :, :, None]
    k_dec = jnp.exp(lg[None, :] * (chunk - 1.0 - pos[:, None]))[None, :, :, None]
    c_dec = jnp.exp(lg * chunk)[None, :, None, None]

    def to_chunks(a):
        return a.astype(jnp.float32).reshape(B, n, chunk, *a.shape[2:]).swapaxes(0, 1)

    def step(s, inp):
        qi, ki, vi = inp
        sc = jnp.einsum('bqhd,bkhd->bhqk', qi, ki) * intra[None]
        o = jnp.einsum('bhqk,bkhe->bqhe', sc, vi)
        o = o + jnp.einsum('bqhd,bhde->bqhe', qi, s) * q_dec
        s = s * c_dec + jnp.einsum('bkhd,bkhe->bhde', ki * k_dec, vi)
        return s, o

    s, o = lax.scan(step, s0.astype(jnp.float32), (to_chunks(q), to_chunks(k), to_chunks(v)))
    o = o.swapaxes(0, 1).reshape(B, T, *o.shape[3:])
    return o, s


def dilated_band_prompt(q, k, v, dil, n_back, slopes):
    B, S, Hg, dh = q.shape
    L = S // dil
    nb = -(-L // BAND_BLOCK)
    lp = nb * BAND_BLOCK

    def sub(a, front):
        a = a.reshape(B, L, dil, Hg, dh)
        return jnp.pad(a, ((0, 0), (front, lp - L), (0, 0), (0, 0), (0, 0)))

    qs = sub(q, 0).reshape(B, nb, BAND_BLOCK, dil, Hg, dh)

    def band_keys(a):
        a = sub(a, BAND_BLOCK).reshape(B, nb + 1, BAND_BLOCK, dil, Hg, dh)
        return jnp.concatenate([a[:, :-1], a[:, 1:]], axis=2)

    ks, vs = band_keys(k), band_keys(v)
    s = jnp.einsum('bnqrhe,bnkrhe->bnrhqk', qs, ks, preferred_element_type=jnp.float32) * (ATT_DH ** -0.5)
    qi = jnp.arange(BAND_BLOCK)[:, None]
    kc = jnp.arange(2 * BAND_BLOCK)[None, :]
    j = qi - kc + BAND_BLOCK
    m_k = jnp.arange(nb)[:, None, None] * BAND_BLOCK + kc[None] - BAND_BLOCK
    valid = (j >= 0) & (j <= n_back) & (m_k >= 0)
    s = s - slopes[:, None, None] * (dil * j).astype(jnp.float32)[None]
    p, lse = masked_softmax_stats(s, valid[None, :, None, None])
    o = jnp.einsum('bnrhqk,bnkrhe->bnqrhe', p, vs.astype(jnp.float32))
    o = o.reshape(B, lp, dil, Hg, dh)[:, :L].reshape(B, S, Hg, dh)
    lse = jnp.transpose(lse, (0, 1, 4, 2, 3)).reshape(B, lp, dil, Hg)[:, :L].reshape(B, S, Hg)
    return o, lse


def dilated_gather_step(q, k, v, kv_buf, win, dil, n_back, slopes):
    Bd, T = q.shape[:2]
    wc = kv_buf.shape[1]
    kf = jnp.concatenate([kv_buf[:, :, 0], k.astype(kv_buf.dtype)], axis=1)
    vf = jnp.concatenate([kv_buf[:, :, 1], v.astype(kv_buf.dtype)], axis=1)
    jj = jnp.arange(n_back + 1)
    idx = wc + jnp.arange(T)[:, None] - dil * jj[None, :]
    valid = idx >= 0
    idx = jnp.maximum(idx, 0)
    kg, vg = kf[:, idx], vf[:, idx]
    s = jnp.einsum('bthe,btjhe->bthj', q, kg, preferred_element_type=jnp.float32) * (ATT_DH ** -0.5)
    s = s - slopes[:, None] * (dil * jj).astype(jnp.float32)[None, :]
    p, lse = masked_softmax_stats(s, valid[None, :, None, :])
    o = jnp.einsum('bthj,btjhe->bthe', p, vg.astype(jnp.float32))
    keep = min(win, wc + T)
    return o, lse, jnp.stack([kf[:, -keep:], vf[:, -keep:]], axis=2)


def token_mixers(h, s_ret0, kv_bufs, w_in, ret_norm_g, q_norm_g, k_norm_g, w_a, w_b, w_o):
    B, T, _ = h.shape
    qr, kr, vr, gr, qa, ka, va, ga, gb = split_in(h @ w_in)
    q = qr.reshape(B, T, RET_HEADS, RET_DK)
    k = kr.reshape(B, T, RET_HEADS, RET_DK) * (RET_DK ** -0.5)
    v = vr.reshape(B, T, RET_HEADS, RET_DV)
    if s_ret0 is None:
        s_ret0 = jnp.zeros((B, RET_HEADS, RET_DK, RET_DV), jnp.float32)
    chunk = RET_CHUNK if T % RET_CHUNK == 0 else T
    o, s_ret = retention_scan(q, k, v, s_ret0, chunk)
    o = rmsnorm(o, ret_norm_g.reshape(RET_HEADS, RET_DV)).reshape(B, T, RET_V_W)
    o_ret = (o * jax.nn.silu(gr.astype(jnp.float32))).astype(h.dtype)
    qa = rmsnorm(qa.reshape(B, T, ATT_HEADS, ATT_DH), q_norm_g)
    ka = rmsnorm(ka.reshape(B, T, ATT_HEADS, ATT_DH), k_norm_g)
    va = va.reshape(B, T, ATT_HEADS, ATT_DH)
    slopes = alibi_slopes()
    outs, lses, new_bufs = [], [], []
    for gi, (win, dil) in enumerate(ATT_GROUPS):
        hs = slice(gi * ATT_HPG, (gi + 1) * ATT_HPG)
        n_back = win // dil
        if kv_bufs is None:
            o_g, l_g = dilated_band_prompt(qa[:, :, hs], ka[:, :, hs], va[:, :, hs], dil, n_back, slopes[hs])
            keep = min(win, T)
            buf = jnp.stack([ka[:, -keep:, hs], va[:, -keep:, hs]], axis=2)
        else:
            o_g, l_g, buf = dilated_gather_step(qa[:, :, hs], ka[:, :, hs], va[:, :, hs], kv_bufs[gi],
                                                win, dil, n_back, slopes[hs])
        outs.append(o_g)
        lses.append(l_g)
        new_bufs.append(buf)
    wgt = jax.nn.softmax(jnp.stack(lses, 0), axis=0)[..., None]
    o_att = jnp.sum(wgt * jnp.stack(outs, 0), axis=0).reshape(B, T, ATT_OUT_W).astype(h.dtype)
    merged = jax.nn.sigmoid(ga) * (o_ret @ w_a) + jax.nn.sigmoid(gb) * (o_att @ w_b)
    return merged @ w_o, s_ret, new_bufs


def moe(h, w_router, b_router, w_up, b_up, w_down, b_down):
    logits = (h @ w_router).astype(jnp.float32) + b_router.astype(jnp.float32)
    top_v, top_i = lax.top_k(logits, TOP_K)
    top_w = jax.nn.softmax(top_v, axis=-1)
    gates = jnp.sum(jax.nn.one_hot(top_i, N_EXPERTS, dtype=jnp.float32) * top_w[..., None], axis=1)
    y = jnp.zeros(h.shape, jnp.float32)
    for e in range(N_EXPERTS):
        u = h @ w_up[e] + b_up[e]
        glu = jnp.minimum(u[:, :D_EXPERT], SWIGLU_LIMIT)
        lin = jnp.clip(u[:, D_EXPERT:], -SWIGLU_LIMIT, SWIGLU_LIMIT)
        a = glu * jax.nn.sigmoid(SWIGLU_ALPHA * glu) * (lin + 1.0)
        y = y + gates[:, e:e + 1] * (a @ w_down[e] + b_down[e]).astype(jnp.float32)
    return y.astype(h.dtype)


def channel_and_ple(x, p, norm_moe_g, w_router, b_router, w_up, b_up, w_down, b_down,
                    norm_ple_g, w_ple_gate, w_ple):
    x = x + moe(rmsnorm(x, norm_moe_g), w_router, b_router, w_up, b_up, w_down, b_down)
    gate = jax.nn.sigmoid(rmsnorm(x, norm_ple_g) @ w_ple_gate)
    return x + gate * (p @ w_ple)


def setup_inputs(seed: int = 0) -> dict:
    key = jax.random.key(seed)
    ks = iter(jax.random.split(key, 40))
    f32 = jnp.float32

    def nrm(shape, scale):
        return jax.random.normal(next(ks), shape, f32) * scale

    def gain(shape):
        return 1.0 + 0.05 * jax.random.normal(next(ks), shape, f32)

    d = {}
    d['x_prompt'] = nrm((BATCH, SEQ, D_MODEL), 1.0)
    d['x_sample'] = nrm((DEC_BATCH, DEC_SEQ, D_MODEL), 1.0)
    d['p_prompt'] = nrm((DEPTH, BATCH, SEQ, PLE_DIM), 1.0)
    d['p_sample'] = nrm((DEPTH, DEC_BATCH, DEC_SEQ, PLE_DIM), 1.0)
    d['state_ret'] = nrm((DEPTH, DEC_BATCH, RET_HEADS, RET_DK, RET_DV), 1.0)
    d['cache_win128_kv'] = nrm((DEPTH, DEC_BATCH, min(128, PAST_LEN), 2, ATT_HPG, ATT_DH), 1.0)
    d['cache_win512_kv'] = nrm((DEPTH, DEC_BATCH, min(512, PAST_LEN), 2, ATT_HPG, ATT_DH), 1.0)
    d['cache_win2048_kv'] = nrm((DEPTH, DEC_BATCH, min(2048, PAST_LEN), 2, ATT_HPG, ATT_DH), 1.0)
    d['norm_mix_g'] = gain((DEPTH, D_MODEL))
    d['w_in'] = nrm((DEPTH, D_MODEL, N_IN), D_MODEL ** -0.5)
    d['ret_norm_g'] = gain((DEPTH, RET_V_W))
    d['q_norm_g'] = gain((DEPTH, ATT_HEADS, ATT_DH))
    d['k_norm_g'] = gain((DEPTH, ATT_HEADS, ATT_DH))
    d['w_a'] = nrm((DEPTH, RET_V_W, D_MODEL), RET_V_W ** -0.5)
    d['w_b'] = nrm((DEPTH, ATT_OUT_W, D_MODEL), ATT_OUT_W ** -0.5)
    d['w_o'] = nrm((DEPTH, D_MODEL, D_MODEL), D_MODEL ** -0.5)
    d['norm_moe_g'] = gain((DEPTH, D_MODEL))
    d['w_router'] = nrm((DEPTH, D_MODEL, N_EXPERTS), D_MODEL ** -0.5)
    d['b_router'] = nrm((DEPTH, N_EXPERTS), 0.01)
    d['w_up'] = nrm((DEPTH, N_EXPERTS, D_MODEL, 2 * D_EXPERT), D_MODEL ** -0.5)
    d['b_up'] = nrm((DEPTH, N_EXPERTS, 2 * D_EXPERT), 0.01)
    d['w_down'] = nrm((DEPTH, N_EXPERTS, D_EXPERT, D_MODEL), D_EXPERT ** -0.5)
    d['b_down'] = nrm((DEPTH, N_EXPERTS, D_MODEL), 0.01)
    d['norm_ple_g'] = gain((DEPTH, D_MODEL))
    d['w_ple_gate'] = nrm((DEPTH, D_MODEL, D_MODEL), D_MODEL ** -0.5)
    d['w_ple'] = nrm((DEPTH, PLE_DIM, D_MODEL), PLE_DIM ** -0.5)
    return d


def reference(x_prompt, x_sample, p_prompt, p_sample, state_ret, cache_win128_kv, cache_win512_kv,
              cache_win2048_kv, norm_mix_g, w_in, ret_norm_g, q_norm_g, k_norm_g, w_a, w_b, w_o,
              norm_moe_g, w_router, b_router, w_up, b_up, w_down, b_down, norm_ple_g, w_ple_gate, w_ple):
    caches = (cache_win128_kv, cache_win512_kv, cache_win2048_kv)
    xp, xs = x_prompt, x_sample
    n_p = BATCH * SEQ
    ret_p, ret_s = [], []
    win_p = [[] for _ in ATT_GROUPS]
    win_s = [[] for _ in ATT_GROUPS]
    for i in range(DEPTH):
        mp, sp, bp = token_mixers(rmsnorm(xp, norm_mix_g[i]), None, None, w_in[i], ret_norm_g[i],
                                  q_norm_g[i], k_norm_g[i], w_a[i], w_b[i], w_o[i])
        ms, ss, bs = token_mixers(rmsnorm(xs, norm_mix_g[i]), state_ret[i], [c[i] for c in caches],
                                  w_in[i], ret_norm_g[i], q_norm_g[i], k_norm_g[i], w_a[i], w_b[i], w_o[i])
        ret_p.append(sp)
        ret_s.append(ss)
        for g in range(len(ATT_GROUPS)):
            win_p[g].append(bp[g])
            win_s[g].append(bs[g])
        rows = jnp.concatenate([(xp + mp).reshape(n_p, D_MODEL), (xs + ms).reshape(-1, D_MODEL)], axis=0)
        prow = jnp.concatenate([p_prompt[i].reshape(n_p, PLE_DIM), p_sample[i].reshape(-1, PLE_DIM)], axis=0)
        rows = channel_and_ple(rows, prow, norm_moe_g[i], w_router[i], b_router[i], w_up[i], b_up[i],
                               w_down[i], b_down[i], norm_ple_g[i], w_ple_gate[i], w_ple[i])
        xp = rows[:n_p].reshape(BATCH, SEQ, D_MODEL)
        xs = rows[n_p:].reshape(DEC_BATCH, DEC_SEQ, D_MODEL)
    return (xp, xs, jnp.stack(ret_p), jnp.stack(ret_s), jnp.stack(win_p[0]), jnp.stack(win_s[0]),
            jnp.stack(win_p[1]), jnp.stack(win_s[1]), jnp.stack(win_p[2]), jnp.stack(win_s[2]))
```

```python
import functools

import jax
import jax.numpy as jnp
from jax import lax
from jax.experimental import pallas as pl
from jax.experimental.pallas import tpu as pltpu

F32 = jnp.float32
BF16 = jnp.bfloat16
I32 = jnp.int32

D_MODEL = 1024
RET_HEADS = 4
RET_DK = 128
RET_DV = 256
RET_QK_W = RET_HEADS * RET_DK
RET_V_W = RET_HEADS * RET_DV
RET_CHUNK = 128
ATT_GROUPS = ((128, 1), (512, 4), (2048, 16))
ATT_HPG = 4
ATT_HEADS = ATT_HPG * len(ATT_GROUPS)
ATT_DH = 64
ATT_W = ATT_HEADS * ATT_DH
ATT_OUT_W = ATT_HPG * ATT_DH
BAND = 128
N_SLABS = ATT_W // 128
SLABS_PER_GROUP = ATT_OUT_W // 128
N_EXPERTS = 32
TOP_K = 4
D_EXPERT = 1024
SWIGLU_LIMIT = 7.0
SWIGLU_ALPHA = 1.702
PLE_DIM = 256
EPS = 1e-6
NEG_INF = -1e30

OFF_QR = 0
OFF_KR = OFF_QR + RET_QK_W
OFF_VR = OFF_KR + RET_QK_W
OFF_GR = OFF_VR + RET_V_W
OFF_QA = OFF_GR + RET_V_W
OFF_KA = OFF_QA + ATT_W
OFF_VA = OFF_KA + ATT_W
OFF_GA = OFF_VA + ATT_W
OFF_GB = OFF_GA + D_MODEL
N_IN = OFF_GB + D_MODEL
RET_W = OFF_QA
GATE_W = 2 * D_MODEL

LANES = 128
SUBLANES = 8
ROW_TILES = D_MODEL // LANES
VMEM_LIMIT = 56 * 1024 * 1024

TM_INPROJ = 256
TM_MERGE = 512
TM_ROUTE = 512
TM_MOE = 256
TM_COMB = 256
RET_CHUNKS_PER_STEP = 4
RET_SAMPLE_BB = 8


def _cparams(sem):
    return pltpu.CompilerParams(dimension_semantics=sem, vmem_limit_bytes=VMEM_LIMIT)


def _dot(a, b):
    return jnp.dot(a, b, preferred_element_type=F32)


def _dot_nt(a, b):
    return lax.dot_general(a, b, (((1,), (1,)), ((), ())), preferred_element_type=F32)


def _dot_tn(a, b):
    return lax.dot_general(a, b, (((0,), (0,)), ((), ())), preferred_element_type=F32)


def _rms(x, g):
    return x * lax.rsqrt(jnp.mean(x * x, axis=-1, keepdims=True) + EPS) * g


def _sigmoid(x):
    return 1.0 / (1.0 + jnp.exp(-x))


def _inproj_kernel(x_ref, g_ref, w_ref, qg_ref, kg_ref, zret_ref, zgate_ref, qa_ref, ka_ref, va_ref):
    h = _rms(x_ref[...], g_ref[...]).astype(BF16)

    def mm(lo, hi):
        return _dot(h, w_ref[:, lo:hi])

    zret_ref[:, OFF_QR:OFF_KR] = mm(OFF_QR, OFF_KR)
    zret_ref[:, OFF_KR:OFF_VR] = mm(OFF_KR, OFF_VR) * (RET_DK ** -0.5)
    zret_ref[:, OFF_VR:OFF_QA] = mm(OFF_VR, OFF_QA)
    zgate_ref[...] = mm(OFF_GA, N_IN)

    lane_lo = lax.broadcasted_iota(I32, (1, LANES), 1) < ATT_DH

    def headnorm(x, g):
        x2 = x * x
        lo = jnp.sum(jnp.where(lane_lo, x2, 0.0), axis=-1, keepdims=True)
        hi = jnp.sum(jnp.where(lane_lo, 0.0, x2), axis=-1, keepdims=True)
        ms = jnp.where(lane_lo, lo, hi) * (1.0 / ATT_DH)
        return x * lax.rsqrt(ms + EPS) * g

    q = mm(OFF_QA, OFF_KA)
    k = mm(OFF_KA, OFF_VA)
    v = mm(OFF_VA, OFF_GA)
    for s in range(N_SLABS):
        sl = slice(s * LANES, (s + 1) * LANES)
        qa_ref[s] = headnorm(q[:, sl], qg_ref[s]) * (ATT_DH ** -0.5)
        ka_ref[s] = headnorm(k[:, sl], kg_ref[s])
        va_ref[s] = v[:, sl]


def _inproj(x, g_mix, w_in_bf, qg, kg):
    n = x.shape[0]
    tm = TM_INPROJ
    slab = jax.ShapeDtypeStruct((N_SLABS, n, LANES), F32)
    slab_spec = pl.BlockSpec((N_SLABS, tm, LANES), lambda i: (0, i, 0))
    return pl.pallas_call(
        _inproj_kernel,
        grid=(n // tm,),
        in_specs=[
            pl.BlockSpec((tm, D_MODEL), lambda i: (i, 0)),
            pl.BlockSpec((1, D_MODEL), lambda i: (0, 0)),
            pl.BlockSpec((D_MODEL, N_IN), lambda i: (0, 0), pipeline_mode=pl.Buffered(1)),
            pl.BlockSpec((N_SLABS, 1, LANES), lambda i: (0, 0, 0)),
            pl.BlockSpec((N_SLABS, 1, LANES), lambda i: (0, 0, 0)),
        ],
        out_specs=[
            pl.BlockSpec((tm, RET_W), lambda i: (i, 0)),
            pl.BlockSpec((tm, GATE_W), lambda i: (i, 0)),
            slab_spec, slab_spec, slab_spec,
        ],
        out_shape=[
            jax.ShapeDtypeStruct((n, RET_W), F32),
            jax.ShapeDtypeStruct((n, GATE_W), F32),
            slab, slab, slab,
        ],
        compiler_params=_cparams(("parallel",)),
        name="inproj",
    )(x, g_mix, w_in_bf, qg, kg)


def _ret_tables(chunk):
    lg = jnp.log1p(-jnp.exp2(-5.0 - jnp.arange(RET_HEADS, dtype=F32)))
    pos = jnp.arange(chunk, dtype=F32)
    rel = pos[:, None] - pos[None, :]
    intra = jnp.where(rel[None] >= 0, jnp.exp(lg[:, None, None] * jnp.maximum(rel, 0.0)[None]), 0.0)
    q_dec = jnp.exp(lg[:, None] * (pos[None, :] + 1.0))
    k_dec = jnp.exp(lg[:, None] * (chunk - 1.0 - pos[None, :]))
    c_dec = jnp.exp(lg * chunk)
    q_dec = jnp.broadcast_to(q_dec[:, :, None], (RET_HEADS, chunk, RET_DV))
    k_dec = jnp.broadcast_to(k_dec[:, :, None], (RET_HEADS, chunk, RET_DK))
    c_dec = jnp.broadcast_to(c_dec[:, None, None], (RET_HEADS, 1, RET_DV))
    return intra, q_dec, k_dec, c_dec


def _ret_head(q, kf, v, gr, s, intra, q_dec, k_dec, c_dec, gn):
    qb = q.astype(BF16)
    vb = v.astype(BF16)
    sc = _dot_nt(qb, kf.astype(BF16)) * intra
    o = _dot(sc.astype(BF16), vb) + _dot(qb, s.astype(BF16)) * q_dec
    s_new = s * c_dec + _dot_tn((kf * k_dec).astype(BF16), vb)
    y = _rms(o, gn) * (gr * _sigmoid(gr))
    return y, s_new


def _ret_prompt_kernel(q_ref, k_ref, v_ref, g_ref, intra_ref, qdec_ref, kdec_ref, cdec_ref, gn_ref,
                       o_ref, sfin_ref, s_scr):
    c_idx = pl.program_id(1)

    @pl.when(c_idx == 0)
    def _():
        s_scr[...] = jnp.zeros_like(s_scr)

    for c in range(RET_CHUNKS_PER_STEP):
        rows = slice(c * RET_CHUNK, (c + 1) * RET_CHUNK)
        for h in range(RET_HEADS):
            ks = slice(h * RET_DK, (h + 1) * RET_DK)
            vs = slice(h * RET_DV, (h + 1) * RET_DV)
            y, s_new = _ret_head(q_ref[rows, ks], k_ref[rows, ks], v_ref[rows, vs], g_ref[rows, vs], s_scr[h],
                                 intra_ref[h], qdec_ref[h], kdec_ref[h], cdec_ref[h], gn_ref[h])
            s_scr[h] = s_new
            o_ref[rows, vs] = y.astype(BF16)

    @pl.when(c_idx == pl.num_programs(1) - 1)
    def _():
        sfin_ref[0] = s_scr[...]


def _ret_prompt(zret, ret_gn, batch, seq):
    tc = RET_CHUNK * RET_CHUNKS_PER_STEP
    nc = seq // tc
    intra, q_dec, k_dec, c_dec = _ret_tables(RET_CHUNK)
    row = lambda b, c: b * nc + c
    const3 = lambda b, c: (0, 0, 0)
    return pl.pallas_call(
        _ret_prompt_kernel,
        grid=(batch, nc),
        in_specs=[
            pl.BlockSpec((tc, RET_QK_W), lambda b, c: (row(b, c), OFF_QR // RET_QK_W)),
            pl.BlockSpec((tc, RET_QK_W), lambda b, c: (row(b, c), OFF_KR // RET_QK_W)),
            pl.BlockSpec((tc, RET_V_W), lambda b, c: (row(b, c), OFF_VR // RET_V_W)),
            pl.BlockSpec((tc, RET_V_W), lambda b, c: (row(b, c), OFF_GR // RET_V_W)),
            pl.BlockSpec(intra.shape, const3),
            pl.BlockSpec(q_dec.shape, const3),
            pl.BlockSpec(k_dec.shape, const3),
            pl.BlockSpec(c_dec.shape, const3),
            pl.BlockSpec((RET_HEADS, 1, RET_DV), const3),
        ],
        out_specs=[
            pl.BlockSpec((tc, RET_V_W), lambda b, c: (row(b, c), 0)),
            pl.BlockSpec((1, RET_HEADS, RET_DK, RET_DV), lambda b, c: (b, 0, 0, 0)),
        ],
        out_shape=[
            jax.ShapeDtypeStruct((batch * seq, RET_V_W), BF16),
            jax.ShapeDtypeStruct((batch, RET_HEADS, RET_DK, RET_DV), F32),
        ],
        scratch_shapes=[pltpu.VMEM((RET_HEADS, RET_DK, RET_DV), F32)],
        compiler_params=_cparams(("parallel", "arbitrary")),
        name="ret_prompt",
    )(zret, zret, zret, zret, intra, q_dec, k_dec, c_dec, ret_gn)


def _ret_sample_kernel(q_ref, k_ref, v_ref, g_ref, s_ref, intra_ref, qdec_ref, kdec_ref, cdec_ref, gn_ref,
                       o_ref, snew_ref, *, t):
    for b in range(RET_SAMPLE_BB):
        rows = slice(b * t, (b + 1) * t)
        for h in range(RET_HEADS):
            ks = slice(h * RET_DK, (h + 1) * RET_DK)
            vs = slice(h * RET_DV, (h + 1) * RET_DV)
            y, s_new = _ret_head(q_ref[rows, ks], k_ref[rows, ks], v_ref[rows, vs], g_ref[rows, vs], s_ref[b, h],
                                 intra_ref[h], qdec_ref[h], kdec_ref[h], cdec_ref[h], gn_ref[h])
            snew_ref[b, h] = s_new
            o_ref[rows, vs] = y.astype(BF16)


def _ret_sample(zret, state, ret_gn, batch, t):
    bb = RET_SAMPLE_BB
    tr = bb * t
    intra, q_dec, k_dec, c_dec = _ret_tables(t)
    const3 = lambda i: (0, 0, 0)
    st_spec = pl.BlockSpec((bb, RET_HEADS, RET_DK, RET_DV), lambda i: (i, 0, 0, 0))
    return pl.pallas_call(
        functools.partial(_ret_sample_kernel, t=t),
        grid=(batch // bb,),
        in_specs=[
            pl.BlockSpec((tr, RET_QK_W), lambda i: (i, OFF_QR // RET_QK_W)),
            pl.BlockSpec((tr, RET_QK_W), lambda i: (i, OFF_KR // RET_QK_W)),
            pl.BlockSpec((tr, RET_V_W), lambda i: (i, OFF_VR // RET_V_W)),
            pl.BlockSpec((tr, RET_V_W), lambda i: (i, OFF_GR // RET_V_W)),
            st_spec,
            pl.BlockSpec(intra.shape, const3),
            pl.BlockSpec(q_dec.shape, const3),
            pl.BlockSpec(k_dec.shape, const3),
            pl.BlockSpec(c_dec.shape, const3),
            pl.BlockSpec((RET_HEADS, 1, RET_DV), const3),
        ],
        out_specs=[pl.BlockSpec((tr, RET_V_W), lambda i: (i, 0)), st_spec],
        out_shape=[
            jax.ShapeDtypeStruct((batch * t, RET_V_W), BF16),
            jax.ShapeDtypeStruct((batch, RET_HEADS, RET_DK, RET_DV), F32),
        ],
        compiler_params=_cparams(("parallel",)),
        name="ret_sample",
    )(zret, zret, zret, zret, state, intra, q_dec, k_dec, c_dec, ret_gn)


def _alibi_slopes():
    return jnp.exp2(-8.0 * (jnp.arange(ATT_HEADS, dtype=F32) + 1.0) / ATT_HEADS)


def _softmax_parts(s):
    m = jnp.max(s, axis=-1, keepdims=True)
    e = jnp.exp(s - m)
    den = jnp.sum(e, axis=-1, keepdims=True)
    return e / den, m + jnp.log(den)


def _attn_prompt_kernel(q_ref, kp_ref, kc_ref, vp_ref, vc_ref, tbl_ref, o_ref, lse_ref):
    lb = pl.program_id(3)
    q = q_ref[0]
    k = jnp.concatenate([kp_ref[0], kc_ref[0]], axis=0).astype(BF16)
    v = jnp.concatenate([vp_ref[0], vc_ref[0]], axis=0).astype(BF16)
    lane_lo = lax.broadcasted_iota(I32, (1, LANES), 1) < ATT_DH
    kcol = lax.broadcasted_iota(I32, (1, 2 * BAND), 1)
    first = jnp.where((lb == 0) & (kcol < BAND), NEG_INF, 0.0)
    outs, lses = [], []
    for hh in range(2):
        keep = lane_lo if hh == 0 else jnp.logical_not(lane_lo)
        qm = jnp.where(keep, q, 0.0).astype(BF16)
        s = _dot_nt(qm, k) + tbl_ref[0, hh] + first
        p, lse = _softmax_parts(s)
        outs.append(_dot(p.astype(BF16), v))
        lses.append(lse)
    o_ref[0] = jnp.where(lane_lo, outs[0], outs[1])
    lse_ref[0] = jnp.where(lane_lo, lses[0], lses[1])


def _attn_prompt(qa, ka, va, group, batch, seq):
    win, dil = ATT_GROUPS[group]
    n_back = win // dil
    n = batch * seq
    nb = seq // dil // BAND
    slopes = _alibi_slopes()[group * ATT_HPG:(group + 1) * ATT_HPG]
    qi = jnp.arange(BAND)[:, None]
    kc = jnp.arange(2 * BAND)[None, :]
    j = qi - kc + BAND
    valid = (j >= 0) & (j <= n_back)
    tbl = jnp.where(valid[None], -slopes[:, None, None] * (dil * j).astype(F32)[None], NEG_INF)
    tbl = tbl.reshape(SLABS_PER_GROUP, 2, BAND, 2 * BAND)

    view = lambda a: a.reshape(a.shape[0], n // dil, dil * LANES)
    s0 = group * SLABS_PER_GROUP
    cur = lambda s, b, r, lb: (s0 + s, b * nb + lb, r)
    prev = lambda s, b, r, lb: (s0 + s, b * nb + jnp.maximum(lb - 1, 0), r)
    blk = (1, BAND, LANES)
    out = jax.ShapeDtypeStruct((SLABS_PER_GROUP, n // dil, dil * LANES), F32)
    out_spec = pl.BlockSpec(blk, lambda s, b, r, lb: (s, b * nb + lb, r))
    o, lse = pl.pallas_call(
        _attn_prompt_kernel,
        grid=(SLABS_PER_GROUP, batch, dil, nb),
        in_specs=[
            pl.BlockSpec(blk, cur), pl.BlockSpec(blk, prev), pl.BlockSpec(blk, cur),
            pl.BlockSpec(blk, prev), pl.BlockSpec(blk, cur),
            pl.BlockSpec((1, 2, BAND, 2 * BAND), lambda s, b, r, lb: (s, 0, 0, 0)),
        ],
        out_specs=[out_spec, out_spec],
        out_shape=[out, out],
        compiler_params=_cparams(("parallel", "parallel", "parallel", "parallel")),
        name=f"attn_prompt_g{group}",
    )(view(qa), view(ka), view(ka), view(va), view(va), tbl)
    return o.reshape(SLABS_PER_GROUP, n, LANES), lse.reshape(SLABS_PER_GROUP, n, LANES)


def _attn_sample_kernel(cache_ref, q_ref, kn_ref, vn_ref, tblc_ref, tbln_ref, newc_ref, o_ref, lse_ref, *, wc, t):
    kw = ATT_OUT_W
    kn = jnp.concatenate([kn_ref[0], kn_ref[1]], axis=1)
    vn = jnp.concatenate([vn_ref[0], vn_ref[1]], axis=1)
    newc_ref[0, pl.ds(0, wc - t), :] = cache_ref[0, pl.ds(t, wc - t), :]
    newc_ref[0, pl.ds(wc - t, t), pl.ds(0, kw)] = kn
    newc_ref[0, pl.ds(wc - t, t), pl.ds(kw, kw)] = vn

    q = jnp.concatenate([q_ref[0], q_ref[1]], axis=1)
    rows = ATT_HPG * t
    rhead = lax.broadcasted_iota(I32, (rows, kw), 0) // t
    lhead = lax.broadcasted_iota(I32, (rows, kw), 1) // ATT_DH
    hm = rhead == lhead
    qb = jnp.where(hm, jnp.concatenate([q] * ATT_HPG, axis=0), 0.0).astype(BF16)
    kc = cache_ref[0, :, pl.ds(0, kw)].astype(BF16)
    vc = cache_ref[0, :, pl.ds(kw, kw)].astype(BF16)
    pad = jnp.zeros((BAND - t, kw), F32)
    knp = jnp.concatenate([kn, pad], axis=0).astype(BF16)
    vnp = jnp.concatenate([vn, pad], axis=0).astype(BF16)
    s_c = _dot_nt(qb, kc) + tblc_ref[...]
    s_n = _dot_nt(qb, knp) + tbln_ref[...]
    m = jnp.maximum(jnp.max(s_c, axis=-1, keepdims=True), jnp.max(s_n, axis=-1, keepdims=True))
    e_c = jnp.exp(s_c - m)
    e_n = jnp.exp(s_n - m)
    den = jnp.sum(e_c, axis=-1, keepdims=True) + jnp.sum(e_n, axis=-1, keepdims=True)
    o = _dot((e_c / den).astype(BF16), vc) + _dot((e_n / den).astype(BF16), vnp)
    lse = m + jnp.log(den)
    o = jnp.where(hm, o, 0.0)
    lse = jnp.where(hm, lse, 0.0)
    o_sel = o[0:t]
    lse_sel = lse[0:t]
    for h in range(1, ATT_HPG):
        o_sel = o_sel + o[h * t:(h + 1) * t]
        lse_sel = lse_sel + lse[h * t:(h + 1) * t]
    for s in range(SLABS_PER_GROUP):
        o_ref[s] = o_sel[:, s * LANES:(s + 1) * LANES]
        lse_ref[s] = lse_sel[:, s * LANES:(s + 1) * LANES]


def _attn_sample(qa, ka, va, cache, group, batch, t):
    win, dil = ATT_GROUPS[group]
    n_back = win // dil
    wc = cache.shape[1]
    assert wc == win and wc % SUBLANES == 0 and t % SUBLANES == 0 and t <= BAND
    kw = ATT_OUT_W
    slopes = _alibi_slopes()[group * ATT_HPG:(group + 1) * ATT_HPG]
    qi = jnp.arange(t)[:, None]
    idx = jnp.concatenate([jnp.arange(wc), wc + jnp.arange(BAND)])[None, :]
    dist = wc + qi - idx
    valid = (dist >= 0) & (dist % dil == 0) & (dist <= dil * n_back) & (idx < wc + t)
    tbl = jnp.where(valid[None], -slopes[:, None, None] * dist.astype(F32)[None], NEG_INF)
    tbl = tbl.reshape(ATT_HPG * t, wc + BAND)
    tblc, tbln = tbl[:, :wc], tbl[:, wc:]

    s0 = group
    new_spec = pl.BlockSpec((SLABS_PER_GROUP, t, LANES), lambda b: (s0, b, 0))
    out_spec = pl.BlockSpec((SLABS_PER_GROUP, t, LANES), lambda b: (0, b, 0))
    cache_spec = pl.BlockSpec((1, wc, 2 * kw), lambda b: (b, 0, 0))
    out = jax.ShapeDtypeStruct((SLABS_PER_GROUP, batch * t, LANES), F32)
    newc, o, lse = pl.pallas_call(
        functools.partial(_attn_sample_kernel, wc=wc, t=t),
        grid=(batch,),
        in_specs=[
            cache_spec, new_spec, new_spec, new_spec,
            pl.BlockSpec(tblc.shape, lambda b: (0, 0)),
            pl.BlockSpec(tbln.shape, lambda b: (0, 0)),
        ],
        out_specs=[cache_spec, out_spec, out_spec],
        out_shape=[jax.ShapeDtypeStruct((batch, wc, 2 * kw), F32), out, out],
        compiler_params=_cparams(("parallel",)),
        name=f"attn_sample_g{group}",
    )(cache.reshape(batch, wc, 2 * kw), qa, ka, va, tblc, tbln)
    return o, lse, newc.reshape(cache.shape)


def _merge_kernel(x_ref, oret_ref, gate_ref, o0_ref, o1_ref, o2_ref, l0_ref, l1_ref, l2_ref,
                  wa_ref, wb_ref, wo_ref, x1_ref):
    slabs = []
    for s in range(SLABS_PER_GROUP):
        ls = [l0_ref[s], l1_ref[s], l2_ref[s]]
        os_ = [o0_ref[s], o1_ref[s], o2_ref[s]]
        m = jnp.maximum(jnp.maximum(ls[0], ls[1]), ls[2])
        es = [jnp.exp(l - m) for l in ls]
        den = es[0] + es[1] + es[2]
        slabs.append((es[0] / den) * os_[0] + (es[1] / den) * os_[1] + (es[2] / den) * os_[2])
    o_att = jnp.concatenate(slabs, axis=1).astype(BF16)
    ga = gate_ref[:, 0:D_MODEL]
    gb = gate_ref[:, D_MODEL:GATE_W]
    merged = _sigmoid(ga) * _dot(oret_ref[...], wa_ref[...]) + _sigmoid(gb) * _dot(o_att, wb_ref[...])
    x1_ref[...] = x_ref[...] + _dot(merged.astype(BF16), wo_ref[...])


def _merge(x, oret, zgate, os_, ls, wa, wb, wo):
    n = x.shape[0]
    tm = min(TM_MERGE, n)
    row = lambda w: pl.BlockSpec((tm, w), lambda i: (i, 0))
    slab = pl.BlockSpec((SLABS_PER_GROUP, tm, LANES), lambda i: (0, i, 0))
    full = lambda a: pl.BlockSpec(a.shape, lambda i: (0, 0))
    return pl.pallas_call(
        _merge_kernel,
        grid=(n // tm,),
        in_specs=[row(D_MODEL), row(RET_V_W), row(GATE_W), slab, slab, slab, slab, slab, slab,
                  full(wa), full(wb), full(wo)],
        out_specs=row(D_MODEL),
        out_shape=jax.ShapeDtypeStruct((n, D_MODEL), F32),
        compiler_params=_cparams(("parallel",)),
        name="merge",
    )(x, oret, zgate, *os_, *ls, wa, wb, wo)


def _router_kernel(xp_ref, xs_ref, g_ref, wr_ref, br_ref, tri_ref, h_ref, idx_ref, rank_ref, w_ref, cnt_ref,
                   run_scr, *, n_prompt_tiles, tm):
    i = pl.program_id(0)

    @pl.when(i == 0)
    def _():
        run_scr[...] = jnp.zeros_like(run_scr)

    x = jnp.where(i < n_prompt_tiles, xp_ref[...], xs_ref[...])
    h = _rms(x, g_ref[...])
    for j in range(ROW_TILES):
        h_ref[pl.ds(j, tm, stride=ROW_TILES), :] = h[:, j * LANES:(j + 1) * LANES]

    logits = jnp.dot(h, wr_ref[...], precision=lax.Precision.HIGHEST, preferred_element_type=F32) + br_ref[...]
    lanes = lax.broadcasted_iota(I32, (1, LANES), 1)
    lanes_f = lanes.astype(F32)
    work = logits
    vals, idxs, hots = [], [], []
    for _ in range(TOP_K):
        m = jnp.max(work, axis=-1, keepdims=True)
        idx = jnp.min(jnp.where(work == m, lanes_f, float(LANES)), axis=-1, keepdims=True)
        hot = lanes_f == idx
        vals.append(m)
        idxs.append(idx)
        hots.append(hot)
        work = jnp.where(hot, -jnp.inf, work)
    exps = [jnp.exp(v - vals[0]) for v in vals]
    den = exps[0] + exps[1] + exps[2] + exps[3]

    cat = jnp.concatenate([hot.astype(BF16) for hot in hots], axis=1)
    cum = _dot(tri_ref[...], cat)
    prev = run_scr[...]
    idx_out = jnp.zeros((tm, LANES), I32)
    rank_out = jnp.zeros((tm, LANES), I32)
    w_out = jnp.zeros((tm, LANES), F32)
    for k in range(TOP_K):
        hot_f = hots[k].astype(F32)
        rank = jnp.sum(hot_f * (cum[:, k * LANES:(k + 1) * LANES] + prev), axis=-1, keepdims=True)
        prev = prev + jnp.sum(hot_f, axis=0, keepdims=True)
        idx_out = jnp.where(lanes == k, idxs[k].astype(I32), idx_out)
        rank_out = jnp.where(lanes == k, rank.astype(I32), rank_out)
        w_out = jnp.where(lanes == k, exps[k] / den, w_out)
    run_scr[...] = prev
    idx_ref[...] = idx_out
    rank_ref[...] = rank_out
    w_ref[...] = w_out
    cnt_ref[...] = prev.astype(I32)


def _router(x1p, x1s, g_moe, wr_pad, br_pad):
    tm = TM_ROUTE
    npt, nst = x1p.shape[0] // tm, x1s.shape[0] // tm
    n = x1p.shape[0] + x1s.shape[0]
    tri = (jnp.arange(tm)[:, None] > jnp.arange(tm)[None, :]).astype(BF16)
    row128 = pl.BlockSpec((tm, LANES), lambda i: (i, 0))
    return pl.pallas_call(
        functools.partial(_router_kernel, n_prompt_tiles=npt, tm=tm),
        grid=(npt + nst,),
        in_specs=[
            pl.BlockSpec((tm, D_MODEL), lambda i: (jnp.minimum(i, npt - 1), 0)),
            pl.BlockSpec((tm, D_MODEL), lambda i: (jnp.clip(i - npt, 0, nst - 1), 0)),
            pl.BlockSpec((1, D_MODEL), lambda i: (0, 0)),
            pl.BlockSpec((D_MODEL, LANES), lambda i: (0, 0)),
            pl.BlockSpec((1, LANES), lambda i: (0, 0)),
            pl.BlockSpec((tm, tm), lambda i: (0, 0)),
        ],
        out_specs=[
            pl.BlockSpec((tm * ROW_TILES, LANES), lambda i: (i, 0)),
            row128, row128, row128,
            pl.BlockSpec((1, LANES), lambda i: (0, 0)),
        ],
        out_shape=[
            jax.ShapeDtypeStruct((n * ROW_TILES, LANES), F32),
            jax.ShapeDtypeStruct((n, LANES), I32),
            jax.ShapeDtypeStruct((n, LANES), I32),
            jax.ShapeDtypeStruct((n, LANES), F32),
            jax.ShapeDtypeStruct((1, LANES), I32),
        ],
        scratch_shapes=[pltpu.VMEM((1, LANES), F32)],
        compiler_params=_cparams(("arbitrary",)),
        name="router",
    )(x1p, x1s, g_moe, wr_pad, br_pad, tri)


def _row_copy(src_ref, src_row, dst_ref, dst_row, sem):
    return pltpu.make_async_copy(
        src_ref.at[pl.ds(pl.multiple_of(src_row * ROW_TILES, ROW_TILES), ROW_TILES), :],
        dst_ref.at[pl.ds(pl.multiple_of(dst_row * ROW_TILES, ROW_TILES), ROW_TILES), :],
        sem)


def _dispatch_kernel(zstart_ref, zlen_ref, nused_ref, slots_ref, h_ref, hs_ref, zero_scr, sem, *, tm, n_tiles):
    i = pl.program_id(0)

    def zero_copy(row, n_rows):
        start = pl.multiple_of(row * ROW_TILES, ROW_TILES)
        return pltpu.make_async_copy(zero_scr.at[pl.ds(0, n_rows * ROW_TILES), :],
                                     hs_ref.at[pl.ds(start, n_rows * ROW_TILES), :], sem)

    def pad_copies(e, op):
        row, left = zstart_ref[e], zlen_ref[e]
        for b in reversed(range(TM_MOE.bit_length() - 1)):
            bit = (left >> b) & 1

            @pl.when(bit == 1)
            def _():
                op(zero_copy(row, 1 << b))

            row = row + (bit << b)

    def all_copies(op):
        lax.fori_loop(0, N_EXPERTS, lambda e, c: (pad_copies(e, op), c)[1], 0)
        lax.fori_loop(nused_ref[0], n_tiles, lambda tt, c: (op(zero_copy(tt * TM_MOE, TM_MOE)), c)[1], 0)

    @pl.when(i == 0)
    def _():
        zero_scr[...] = jnp.zeros_like(zero_scr)
        all_copies(lambda c: c.start())
        all_copies(lambda c: c.wait())

    def body(tok, carry):
        for k in range(TOP_K):
            _row_copy(h_ref, tok, hs_ref, slots_ref[0, 0, tok * TOP_K + k], sem).start()
        return carry

    lax.fori_loop(0, tm, body, 0)
    for _ in range(TOP_K):
        pltpu.make_async_copy(h_ref, hs_ref.at[pl.ds(0, tm * ROW_TILES), :], sem).wait()


def _dispatch(hrow, slots, zstart, zlen, n_used, n_tiles):
    tm = TM_ROUTE
    n = hrow.shape[0] // ROW_TILES
    nt = n // tm
    slots3 = slots.reshape(nt, 1, tm * TOP_K)
    grid_spec = pltpu.PrefetchScalarGridSpec(
        num_scalar_prefetch=3,
        grid=(nt,),
        in_specs=[
            pl.BlockSpec((1, 1, tm * TOP_K), lambda i, z, l, u: (i, 0, 0), memory_space=pltpu.SMEM),
            pl.BlockSpec((tm * ROW_TILES, LANES), lambda i, z, l, u: (i, 0)),
        ],
        out_specs=pl.BlockSpec(memory_space=pl.ANY),
        scratch_shapes=[pltpu.VMEM((TM_MOE * ROW_TILES, LANES), F32), pltpu.SemaphoreType.DMA(())],
    )
    return pl.pallas_call(
        functools.partial(_dispatch_kernel, tm=tm, n_tiles=n_tiles),
        grid_spec=grid_spec,
        out_shape=jax.ShapeDtypeStruct((n_tiles * TM_MOE * ROW_TILES, LANES), F32),
        compiler_params=_cparams(("arbitrary",)),
        name="dispatch",
    )(zstart, zlen, n_used, slots3, hrow)


def _experts_kernel(texp_ref, nused_ref, hs_ref, wu_ref, bu_ref, wd_ref, bd_ref, ys_ref, wu_scr, wd_scr):
    j = pl.program_id(0)
    tm = TM_MOE
    n_used = nused_ref[0]
    jj = jnp.minimum(j, n_used - 1)
    new_expert = (j == 0) | (texp_ref[jj] != texp_ref[jnp.maximum(jj - 1, 0)])

    @pl.when((j < n_used) & new_expert)
    def _():
        wu_scr[...] = wu_ref[0].astype(BF16)
        wd_scr[...] = wd_ref[0].astype(BF16)

    @pl.when(j < n_used)
    def _():
        x = jnp.concatenate(
            [hs_ref[pl.ds(c, tm, stride=ROW_TILES), :] for c in range(ROW_TILES)], axis=1).astype(BF16)
        u = _dot(x, wu_scr[...]) + bu_ref[0]
        glu = jnp.minimum(u[:, :D_EXPERT], SWIGLU_LIMIT)
        lin = jnp.clip(u[:, D_EXPERT:], -SWIGLU_LIMIT, SWIGLU_LIMIT)
        a = glu * _sigmoid(SWIGLU_ALPHA * glu) * (lin + 1.0)
        y = _dot(a.astype(BF16), wd_scr[...]) + bd_ref[0]
        for c in range(ROW_TILES):
            ys_ref[pl.ds(c, tm, stride=ROW_TILES), :] = y[:, c * LANES:(c + 1) * LANES]

    @pl.when(j >= n_used)
    def _():
        ys_ref[...] = jnp.zeros_like(ys_ref)


def _experts(hs, tile_expert, n_used, w_up, b_up, w_down, b_down, n_tiles):
    tm = TM_MOE
    tile = lambda j, te, nu: (jnp.minimum(j, nu[0] - 1), 0)
    exp3 = lambda j, te, nu: (te[jnp.minimum(j, nu[0] - 1)], 0, 0)
    grid_spec = pltpu.PrefetchScalarGridSpec(
        num_scalar_prefetch=2,
        grid=(n_tiles,),
        in_specs=[
            pl.BlockSpec((tm * ROW_TILES, LANES), tile),
            pl.BlockSpec((1, D_MODEL, 2 * D_EXPERT), exp3),
            pl.BlockSpec((1, 1, 2 * D_EXPERT), exp3),
            pl.BlockSpec((1, D_EXPERT, D_MODEL), exp3),
            pl.BlockSpec((1, 1, D_MODEL), exp3),
        ],
        out_specs=pl.BlockSpec((tm * ROW_TILES, LANES), lambda j, te, nu: (j, 0)),
        scratch_shapes=[pltpu.VMEM((D_MODEL, 2 * D_EXPERT), BF16), pltpu.VMEM((D_EXPERT, D_MODEL), BF16)],
    )
    return pl.pallas_call(
        _experts_kernel,
        grid_spec=grid_spec,
        out_shape=jax.ShapeDtypeStruct((n_tiles * tm * ROW_TILES, LANES), F32),
        compiler_params=_cparams(("arbitrary",)),
        name="experts",
    )(tile_expert, n_used, hs, w_up, b_up.reshape(N_EXPERTS, 1, -1), w_down, b_down.reshape(N_EXPERTS, 1, -1))


def _combine_kernel(slots_ref, x1_ref, w_ref, p_ref, g_ref, wpg_ref, wpe_ref, ys_ref, y_ref, buf, sem, *, tm):
    def body(tok, carry):
        for k in range(TOP_K):
            _row_copy(ys_ref, slots_ref[0, 0, tok * TOP_K + k], buf.at[k], tok, sem).start()
        return carry

    lax.fori_loop(0, tm, body, 0)
    for k in range(TOP_K):
        pltpu.make_async_copy(ys_ref.at[pl.ds(0, tm * ROW_TILES), :], buf.at[k], sem).wait()

    w = w_ref[...]
    wk = [jnp.broadcast_to(w[:, k:k + 1], (tm, LANES)) for k in range(TOP_K)]
    cols = []
    for c in range(ROW_TILES):
        acc = jnp.zeros((tm, LANES), F32)
        for k in range(TOP_K):
            acc = acc + wk[k] * buf[k, pl.ds(c, tm, stride=ROW_TILES), :]
        cols.append(acc)
    x2 = x1_ref[...] + jnp.concatenate(cols, axis=1)
    gate = _sigmoid(_dot(_rms(x2, g_ref[...]).astype(BF16), wpg_ref[...]))
    y_ref[...] = x2 + gate * _dot(p_ref[...].astype(BF16), wpe_ref[...])


def _combine(x1, slots, topw, p, g_ple, wpg, wpe, ys):
    n = x1.shape[0]
    tm = TM_COMB
    nt = n // tm
    slots3 = slots.reshape(nt, 1, tm * TOP_K)
    full = lambda a: pl.BlockSpec(a.shape, lambda i: (0, 0))
    return pl.pallas_call(
        functools.partial(_combine_kernel, tm=tm),
        grid=(nt,),
        in_specs=[
            pl.BlockSpec((1, 1, tm * TOP_K), lambda i: (i, 0, 0), memory_space=pltpu.SMEM),
            pl.BlockSpec((tm, D_MODEL), lambda i: (i, 0)),
            pl.BlockSpec((tm, LANES), lambda i: (i, 0)),
            pl.BlockSpec((tm, PLE_DIM), lambda i: (i, 0)),
            full(g_ple), full(wpg), full(wpe),
            pl.BlockSpec(memory_space=pl.ANY),
        ],
        out_specs=pl.BlockSpec((tm, D_MODEL), lambda i: (i, 0)),
        out_shape=jax.ShapeDtypeStruct((n, D_MODEL), F32),
        scratch_shapes=[pltpu.VMEM((TOP_K, tm * ROW_TILES, LANES), F32), pltpu.SemaphoreType.DMA(())],
        compiler_params=_cparams(("arbitrary",)),
        name="combine",
    )(slots3, x1, topw, p, g_ple, wpg, wpe, ys)


def _cache_from_slabs(ka, va, group, batch, seq, keep):
    s0 = group * SLABS_PER_GROUP

    def tail(a):
        a = a[s0:s0 + SLABS_PER_GROUP].reshape(SLABS_PER_GROUP, batch, seq, LANES)[:, :, seq - keep:]
        return jnp.transpose(a, (1, 2, 0, 3)).reshape(batch, keep, ATT_HPG, ATT_DH)

    return jnp.stack([tail(ka), tail(va)], axis=2)


def _layer(xp, xs, pp, ps, state, caches, g_mix, w_in, ret_gn, qn_g, kn_g, w_a, w_b, w_o, g_moe, w_router, b_router,
           w_up, b_up, w_down, b_down, g_ple, w_ple_gate, w_ple):
    batch, seq, _ = xp.shape
    dbatch, dseq, _ = xs.shape
    n_p, n_s = batch * seq, dbatch * dseq
    n = n_p + n_s

    w_in_bf = w_in.astype(BF16)
    g_mix2 = g_mix.reshape(1, D_MODEL)
    qg = qn_g.reshape(N_SLABS, 1, LANES)
    kg = kn_g.reshape(N_SLABS, 1, LANES)
    gn = ret_gn.reshape(RET_HEADS, 1, RET_DV)
    wa, wb, wo = w_a.astype(BF16), w_b.astype(BF16), w_o.astype(BF16)

    xp2 = xp.reshape(n_p, D_MODEL)
    xs2 = xs.reshape(n_s, D_MODEL)
    zret_p, zgate_p, qa_p, ka_p, va_p = _inproj(xp2, g_mix2, w_in_bf, qg, kg)
    zret_s, zgate_s, qa_s, ka_s, va_s = _inproj(xs2, g_mix2, w_in_bf, qg, kg)

    oret_p, state_p = _ret_prompt(zret_p, gn, batch, seq)
    oret_s, state_s = _ret_sample(zret_s, state, gn, dbatch, dseq)

    o_p, l_p, o_s, l_s, win_p, win_s = [], [], [], [], [], []
    for g, (win, _) in enumerate(ATT_GROUPS):
        o, l = _attn_prompt(qa_p, ka_p, va_p, g, batch, seq)
        o_p.append(o)
        l_p.append(l)
        win_p.append(_cache_from_slabs(ka_p, va_p, g, batch, seq, min(win, seq)))
        o, l, newc = _attn_sample(qa_s, ka_s, va_s, caches[g], g, dbatch, dseq)
        o_s.append(o)
        l_s.append(l)
        win_s.append(newc)

    x1p = _merge(xp2, oret_p, zgate_p, o_p, l_p, wa, wb, wo)
    x1s = _merge(xs2, oret_s, zgate_s, o_s, l_s, wa, wb, wo)

    wr_pad = jnp.zeros((D_MODEL, LANES), F32).at[:, :N_EXPERTS].set(w_router)
    br_pad = jnp.full((1, LANES), NEG_INF, F32).at[0, :N_EXPERTS].set(b_router)
    hrow, top_i, rank, top_w, counts = _router(x1p, x1s, g_moe.reshape(1, D_MODEL), wr_pad, br_pad)

    cnt = counts[0, :N_EXPERTS]
    padded = (cnt + TM_MOE - 1) // TM_MOE * TM_MOE
    ends = jnp.cumsum(padded)
    goff = ends - padded
    slots = goff[top_i[:, :TOP_K]] + rank[:, :TOP_K]
    n_tiles = (n * TOP_K) // TM_MOE + N_EXPERTS
    tile_ends = ends // TM_MOE
    n_used = tile_ends[-1:].astype(I32)
    tile_expert = jnp.minimum(jnp.searchsorted(tile_ends, jnp.arange(n_tiles), side="right"), N_EXPERTS - 1).astype(I32)
    zstart = (goff + cnt).astype(I32)
    zlen = (padded - cnt).astype(I32)

    hs = _dispatch(hrow, slots, zstart, zlen, n_used, n_tiles)
    ys = _experts(hs, tile_expert, n_used, w_up, b_up, w_down, b_down, n_tiles)

    g_ple2 = g_ple.reshape(1, D_MODEL)
    wpg, wpe = w_ple_gate.astype(BF16), w_ple.astype(BF16)
    yp = _combine(x1p, slots[:n_p], top_w[:n_p], pp.reshape(n_p, PLE_DIM), g_ple2, wpg, wpe, ys)
    ys_out = _combine(x1s, slots[n_p:], top_w[n_p:], ps.reshape(n_s, PLE_DIM), g_ple2, wpg, wpe, ys)
    return (yp.reshape(batch, seq, D_MODEL), ys_out.reshape(dbatch, dseq, D_MODEL), state_p, state_s, win_p, win_s)


def kernel(x_prompt, x_sample, p_prompt, p_sample, state_ret, cache_win128_kv, cache_win512_kv, cache_win2048_kv, norm_mix_g, w_in, ret_norm_g, q_norm_g, k_norm_g, w_a, w_b, w_o, norm_moe_g, w_router, b_router, w_up, b_up, w_down, b_down, norm_ple_g, w_ple_gate, w_ple):
    caches = (cache_win128_kv, cache_win512_kv, cache_win2048_kv)
    depth = w_in.shape[0]
    xp, xs = x_prompt, x_sample
    ret_p, ret_s = [], []
    win_p = [[] for _ in ATT_GROUPS]
    win_s = [[] for _ in ATT_GROUPS]
    for i in range(depth):
        xp, xs, sp, ss, bp, bs = _layer(
            xp, xs, p_prompt[i], p_sample[i], state_ret[i], [c[i] for c in caches], norm_mix_g[i], w_in[i],
            ret_norm_g[i], q_norm_g[i], k_norm_g[i], w_a[i], w_b[i], w_o[i], norm_moe_g[i], w_router[i], b_router[i],
            w_up[i], b_up[i], w_down[i], b_down[i], norm_ple_g[i], w_ple_gate[i], w_ple[i])
        ret_p.append(sp)
        ret_s.append(ss)
        for g in range(len(ATT_GROUPS)):
            win_p[g].append(bp[g])
            win_s[g].append(bs[g])
    return (xp, xs, jnp.stack(ret_p), jnp.stack(ret_s), jnp.stack(win_p[0]), jnp.stack(win_s[0]),
            jnp.stack(win_p[1]), jnp.stack(win_s[1]), jnp.stack(win_p[2]), jnp.stack(win_s[2]))
```

```python
import functools

import jax
import jax.numpy as jnp
from jax import lax
from jax.experimental import pallas as pl
from jax.experimental.pallas import tpu as pltpu

F32 = jnp.float32
BF16 = jnp.bfloat16
I32 = jnp.int32

D_MODEL = 1024
RET_HEADS = 4
RET_DK = 128
RET_DV = 256
RET_QK_W = RET_HEADS * RET_DK
RET_V_W = RET_HEADS * RET_DV
RET_CHUNK = 128
ATT_GROUPS = ((128, 1), (512, 4), (2048, 16))
ATT_HPG = 4
ATT_HEADS = ATT_HPG * len(ATT_GROUPS)
ATT_DH = 64
ATT_W = ATT_HEADS * ATT_DH
ATT_OUT_W = ATT_HPG * ATT_DH
BAND = 128
N_SLABS = ATT_W // 128
SLABS_PER_GROUP = ATT_OUT_W // 128
N_EXPERTS = 32
TOP_K = 4
D_EXPERT = 1024
SWIGLU_LIMIT = 7.0
SWIGLU_ALPHA = 1.702
PLE_DIM = 256
EPS = 1e-6
NEG_INF = -1e30

OFF_QR = 0
OFF_KR = OFF_QR + RET_QK_W
OFF_VR = OFF_KR + RET_QK_W
OFF_GR = OFF_VR + RET_V_W
OFF_QA = OFF_GR + RET_V_W
OFF_KA = OFF_QA + ATT_W
OFF_VA = OFF_KA + ATT_W
OFF_GA = OFF_VA + ATT_W
OFF_GB = OFF_GA + D_MODEL
N_IN = OFF_GB + D_MODEL
RET_W = OFF_QA
GATE_W = 2 * D_MODEL

LANES = 128
SUBLANES = 8
ROW_TILES = D_MODEL // LANES
VMEM_LIMIT = 56 * 1024 * 1024

TM_INPROJ = 256
TM_MERGE = 512
TM_ROUTE = 512
TM_MOE = 256
TM_COMB = 256
RET_CHUNKS_PER_STEP = 4
RET_SAMPLE_BB = 8
ATTN_Q_BLOCKS = 4
ATTN_SAMPLE_BLOCK_BYTES = 4 * 1024 * 1024
ATTN_SAMPLE_MAX_BB = 8


def _cparams(sem):
    return pltpu.CompilerParams(dimension_semantics=sem, vmem_limit_bytes=VMEM_LIMIT)


def _dot(a, b):
    return jnp.dot(a, b, preferred_element_type=F32)


def _dot_nt(a, b):
    return lax.dot_general(a, b, (((1,), (1,)), ((), ())), preferred_element_type=F32)


def _dot_tn(a, b):
    return lax.dot_general(a, b, (((0,), (0,)), ((), ())), preferred_element_type=F32)


def _rms(x, g):
    return x * lax.rsqrt(jnp.mean(x * x, axis=-1, keepdims=True) + EPS) * g


def _sigmoid(x):
    return 1.0 / (1.0 + jnp.exp(-x))


def _dilated_cols(dil):
    return dil * LANES


def _inproj_kernel(x_ref, g_ref, w_ref, qg_ref, kg_ref, zret_ref, zgate_ref, *rest, dilated, tm):
    h = _rms(x_ref[...], g_ref[...]).astype(BF16)

    def mm(lo, hi):
        return _dot(h, w_ref[:, lo:hi])

    zret_ref[:, OFF_QR:OFF_KR] = mm(OFF_QR, OFF_KR)
    zret_ref[:, OFF_KR:OFF_VR] = mm(OFF_KR, OFF_VR) * (RET_DK ** -0.5)
    zret_ref[:, OFF_VR:OFF_QA] = mm(OFF_VR, OFF_QA)
    zgate_ref[...] = mm(OFF_GA, N_IN)

    lane_lo = lax.broadcasted_iota(I32, (1, LANES), 1) < ATT_DH

    def headnorm(x, g):
        x2 = x * x
        lo = jnp.sum(jnp.where(lane_lo, x2, 0.0), axis=-1, keepdims=True)
        hi = jnp.sum(jnp.where(lane_lo, 0.0, x2), axis=-1, keepdims=True)
        ms = jnp.where(lane_lo, lo, hi) * (1.0 / ATT_DH)
        return x * lax.rsqrt(ms + EPS) * g

    q = mm(OFF_QA, OFF_KA)
    k = mm(OFF_KA, OFF_VA)
    v = mm(OFF_VA, OFF_GA)
    for s in range(N_SLABS):
        sl = slice(s * LANES, (s + 1) * LANES)
        vals = (headnorm(q[:, sl], qg_ref[s]) * (ATT_DH ** -0.5), headnorm(k[:, sl], kg_ref[s]), v[:, sl])
        g, sg = divmod(s, SLABS_PER_GROUP)
        dil = ATT_GROUPS[g][1]
        for a, val in enumerate(vals):
            if not dilated:
                rest[a][s] = val
            elif dil == 1:
                rest[3 * g + a][sg] = val
            else:
                scr = rest[-1]
                slot = (s - SLABS_PER_GROUP) * 3 + a
                scr[slot] = val
                for r in range(dil):
                    rest[3 * g + a][sg, :, r * LANES:(r + 1) * LANES] = scr[slot, pl.ds(r, tm // dil, stride=dil), :]


def _inproj(x, g_mix, w_in_bf, qg, kg, dilated):
    n = x.shape[0]
    tm = TM_INPROJ
    if dilated:
        slab_shapes, slab_specs = [], []
        for _, dil in ATT_GROUPS:
            assert tm % (dil * SUBLANES) == 0
            slab_shapes += [jax.ShapeDtypeStruct((SLABS_PER_GROUP, n // dil, _dilated_cols(dil)), F32)] * 3
            slab_specs += [pl.BlockSpec((SLABS_PER_GROUP, tm // dil, _dilated_cols(dil)), lambda i: (0, i, 0))] * 3
        scratch = [pltpu.VMEM(((N_SLABS - SLABS_PER_GROUP) * 3, tm, LANES), F32)]
    else:
        slab_shapes = [jax.ShapeDtypeStruct((N_SLABS, n, LANES), F32)] * 3
        slab_specs = [pl.BlockSpec((N_SLABS, tm, LANES), lambda i: (0, i, 0))] * 3
        scratch = []
    return pl.pallas_call(
        functools.partial(_inproj_kernel, dilated=dilated, tm=tm),
        grid=(n // tm,),
        in_specs=[
            pl.BlockSpec((tm, D_MODEL), lambda i: (i, 0)),
            pl.BlockSpec((1, D_MODEL), lambda i: (0, 0)),
            pl.BlockSpec((D_MODEL, N_IN), lambda i: (0, 0), pipeline_mode=pl.Buffered(1)),
            pl.BlockSpec((N_SLABS, 1, LANES), lambda i: (0, 0, 0)),
            pl.BlockSpec((N_SLABS, 1, LANES), lambda i: (0, 0, 0)),
        ],
        out_specs=[pl.BlockSpec((tm, RET_W), lambda i: (i, 0)), pl.BlockSpec((tm, GATE_W), lambda i: (i, 0))]
        + slab_specs,
        out_shape=[jax.ShapeDtypeStruct((n, RET_W), F32), jax.ShapeDtypeStruct((n, GATE_W), F32)] + slab_shapes,
        scratch_shapes=scratch,
        compiler_params=_cparams(("parallel",)),
        name="inproj_dilated" if dilated else "inproj",
    )(x, g_mix, w_in_bf, qg, kg)


def _ret_tables(chunk):
    lg = jnp.log1p(-jnp.exp2(-5.0 - jnp.arange(RET_HEADS, dtype=F32)))
    pos = jnp.arange(chunk, dtype=F32)
    rel = pos[:, None] - pos[None, :]
    intra = jnp.where(rel[None] >= 0, jnp.exp(lg[:, None, None] * jnp.maximum(rel, 0.0)[None]), 0.0)
    q_dec = jnp.exp(lg[:, None] * (pos[None, :] + 1.0))
    k_dec = jnp.exp(lg[:, None] * (chunk - 1.0 - pos[None, :]))
    c_dec = jnp.exp(lg * chunk)
    q_dec = jnp.broadcast_to(q_dec[:, :, None], (RET_HEADS, chunk, RET_DV))
    k_dec = jnp.broadcast_to(k_dec[:, :, None], (RET_HEADS, chunk, RET_DK))
    c_dec = jnp.broadcast_to(c_dec[:, None, None], (RET_HEADS, 1, RET_DV))
    return intra, q_dec, k_dec, c_dec


def _ret_head(q, kf, v, gr, s, intra, q_dec, k_dec, c_dec, gn):
    qb = q.astype(BF16)
    vb = v.astype(BF16)
    sc = _dot_nt(qb, kf.astype(BF16)) * intra
    o = _dot(sc.astype(BF16), vb) + _dot(qb, s.astype(BF16)) * q_dec
    s_new = s * c_dec + _dot_tn((kf * k_dec).astype(BF16), vb)
    y = _rms(o, gn) * (gr * _sigmoid(gr))
    return y, s_new


def _ret_prompt_kernel(q_ref, k_ref, v_ref, g_ref, intra_ref, qdec_ref, kdec_ref, cdec_ref, gn_ref,
                       o_ref, sfin_ref, s_scr):
    c_idx = pl.program_id(1)

    @pl.when(c_idx == 0)
    def _():
        s_scr[...] = jnp.zeros_like(s_scr)

    for c in range(RET_CHUNKS_PER_STEP):
        rows = slice(c * RET_CHUNK, (c + 1) * RET_CHUNK)
        for h in range(RET_HEADS):
            ks = slice(h * RET_DK, (h + 1) * RET_DK)
            vs = slice(h * RET_DV, (h + 1) * RET_DV)
            y, s_new = _ret_head(q_ref[rows, ks], k_ref[rows, ks], v_ref[rows, vs], g_ref[rows, vs], s_scr[h],
                                 intra_ref[h], qdec_ref[h], kdec_ref[h], cdec_ref[h], gn_ref[h])
            s_scr[h] = s_new
            o_ref[rows, vs] = y.astype(BF16)

    @pl.when(c_idx == pl.num_programs(1) - 1)
    def _():
        sfin_ref[0] = s_scr[...]


def _ret_prompt(zret, ret_gn, batch, seq):
    tc = RET_CHUNK * RET_CHUNKS_PER_STEP
    nc = seq // tc
    intra, q_dec, k_dec, c_dec = _ret_tables(RET_CHUNK)
    row = lambda b, c: b * nc + c
    const3 = lambda b, c: (0, 0, 0)
    return pl.pallas_call(
        _ret_prompt_kernel,
        grid=(batch, nc),
        in_specs=[
            pl.BlockSpec((tc, RET_QK_W), lambda b, c: (row(b, c), OFF_QR // RET_QK_W)),
            pl.BlockSpec((tc, RET_QK_W), lambda b, c: (row(b, c), OFF_KR // RET_QK_W)),
            pl.BlockSpec((tc, RET_V_W), lambda b, c: (row(b, c), OFF_VR // RET_V_W)),
            pl.BlockSpec((tc, RET_V_W), lambda b, c: (row(b, c), OFF_GR // RET_V_W)),
            pl.BlockSpec(intra.shape, const3),
            pl.BlockSpec(q_dec.shape, const3),
            pl.BlockSpec(k_dec.shape, const3),
            pl.BlockSpec(c_dec.shape, const3),
            pl.BlockSpec((RET_HEADS, 1, RET_DV), const3),
        ],
        out_specs=[
            pl.BlockSpec((tc, RET_V_W), lambda b, c: (row(b, c), 0)),
            pl.BlockSpec((1, RET_HEADS, RET_DK, RET_DV), lambda b, c: (b, 0, 0, 0)),
        ],
        out_shape=[
            jax.ShapeDtypeStruct((batch * seq, RET_V_W), BF16),
            jax.ShapeDtypeStruct((batch, RET_HEADS, RET_DK, RET_DV), F32),
        ],
        scratch_shapes=[pltpu.VMEM((RET_HEADS, RET_DK, RET_DV), F32)],
        compiler_params=_cparams(("parallel", "arbitrary")),
        name="ret_prompt",
    )(zret, zret, zret, zret, intra, q_dec, k_dec, c_dec, ret_gn)


def _ret_sample_kernel(q_ref, k_ref, v_ref, g_ref, s_ref, intra_ref, qdec_ref, kdec_ref, cdec_ref, gn_ref,
                       o_ref, snew_ref, *, t):
    for b in range(RET_SAMPLE_BB):
        rows = slice(b * t, (b + 1) * t)
        for h in range(RET_HEADS):
            ks = slice(h * RET_DK, (h + 1) * RET_DK)
            vs = slice(h * RET_DV, (h + 1) * RET_DV)
            y, s_new = _ret_head(q_ref[rows, ks], k_ref[rows, ks], v_ref[rows, vs], g_ref[rows, vs], s_ref[b, h],
                                 intra_ref[h], qdec_ref[h], kdec_ref[h], cdec_ref[h], gn_ref[h])
            snew_ref[b, h] = s_new
            o_ref[rows, vs] = y.astype(BF16)


def _ret_sample(zret, state, ret_gn, batch, t):
    bb = RET_SAMPLE_BB
    tr = bb * t
    intra, q_dec, k_dec, c_dec = _ret_tables(t)
    const3 = lambda i: (0, 0, 0)
    st_spec = pl.BlockSpec((bb, RET_HEADS, RET_DK, RET_DV), lambda i: (i, 0, 0, 0))
    return pl.pallas_call(
        functools.partial(_ret_sample_kernel, t=t),
        grid=(batch // bb,),
        in_specs=[
            pl.BlockSpec((tr, RET_QK_W), lambda i: (i, OFF_QR // RET_QK_W)),
            pl.BlockSpec((tr, RET_QK_W), lambda i: (i, OFF_KR // RET_QK_W)),
            pl.BlockSpec((tr, RET_V_W), lambda i: (i, OFF_VR // RET_V_W)),
            pl.BlockSpec((tr, RET_V_W), lambda i: (i, OFF_GR // RET_V_W)),
            st_spec,
            pl.BlockSpec(intra.shape, const3),
            pl.BlockSpec(q_dec.shape, const3),
            pl.BlockSpec(k_dec.shape, const3),
            pl.BlockSpec(c_dec.shape, const3),
            pl.BlockSpec((RET_HEADS, 1, RET_DV), const3),
        ],
        out_specs=[pl.BlockSpec((tr, RET_V_W), lambda i: (i, 0)), st_spec],
        out_shape=[
            jax.ShapeDtypeStruct((batch * t, RET_V_W), BF16),
            jax.ShapeDtypeStruct((batch, RET_HEADS, RET_DK, RET_DV), F32),
        ],
        compiler_params=_cparams(("parallel",)),
        name="ret_sample",
    )(zret, zret, zret, zret, state, intra, q_dec, k_dec, c_dec, ret_gn)


def _alibi_slopes():
    return jnp.exp2(-8.0 * (jnp.arange(ATT_HEADS, dtype=F32) + 1.0) / ATT_HEADS)


def _softmax_parts(s):
    m = jnp.max(s, axis=-1, keepdims=True)
    e = jnp.exp(s - m)
    den = jnp.sum(e, axis=-1, keepdims=True)
    return e / den, m + jnp.log(den)


def _attn_prompt_kernel(q_ref, kp_ref, kc_ref, vp_ref, vc_ref, tbl_ref, o_ref, lse_ref, *, qb):
    i = pl.program_id(3)
    k_all = jnp.concatenate([kp_ref[0], kc_ref[0]], axis=0).astype(BF16)
    v_all = jnp.concatenate([vp_ref[0], vc_ref[0]], axis=0).astype(BF16)
    lane_lo = lax.broadcasted_iota(I32, (1, LANES), 1) < ATT_DH
    kcol = lax.broadcasted_iota(I32, (1, 2 * BAND), 1)
    first = jnp.where((i == 0) & (kcol < BAND), NEG_INF, 0.0)
    for j in range(qb):
        rows = slice(j * BAND, (j + 1) * BAND)
        q = q_ref[0, rows, :]
        k = k_all[j * BAND:(j + 2) * BAND]
        v = v_all[j * BAND:(j + 2) * BAND]
        outs, lses = [], []
        for hh in range(2):
            keep = lane_lo if hh == 0 else jnp.logical_not(lane_lo)
            qm = jnp.where(keep, q, 0.0).astype(BF16)
            s = _dot_nt(qm, k) + tbl_ref[0, hh]
            if j == 0:
                s = s + first
            p, lse = _softmax_parts(s)
            outs.append(_dot(p.astype(BF16), v))
            lses.append(lse)
        o_ref[0, rows, :] = jnp.where(lane_lo, outs[0], outs[1])
        lse_ref[0, rows, :] = jnp.where(lane_lo, lses[0], lses[1])


def _attn_prompt(qa, ka, va, group, batch, seq):
    win, dil = ATT_GROUPS[group]
    n_back = win // dil
    nb = seq // dil // BAND
    qb = min(ATTN_Q_BLOCKS, nb)
    nbq = nb // qb
    slopes = _alibi_slopes()[group * ATT_HPG:(group + 1) * ATT_HPG]
    qi = jnp.arange(BAND)[:, None]
    kc = jnp.arange(2 * BAND)[None, :]
    j = qi - kc + BAND
    valid = (j >= 0) & (j <= n_back)
    tbl = jnp.where(valid[None], -slopes[:, None, None] * (dil * j).astype(F32)[None], NEG_INF)
    tbl = tbl.reshape(SLABS_PER_GROUP, 2, BAND, 2 * BAND)

    cur = lambda s, b, r, i: (s, b * nbq + i, r)
    prev = lambda s, b, r, i: (s, b * nb + jnp.maximum(i * qb - 1, 0), r)
    cur_blk = pl.BlockSpec((1, qb * BAND, LANES), cur)
    prev_blk = pl.BlockSpec((1, BAND, LANES), prev)
    out = jax.ShapeDtypeStruct(qa.shape, F32)
    return pl.pallas_call(
        functools.partial(_attn_prompt_kernel, qb=qb),
        grid=(SLABS_PER_GROUP, batch, dil, nbq),
        in_specs=[
            cur_blk, prev_blk, cur_blk, prev_blk, cur_blk,
            pl.BlockSpec((1, 2, BAND, 2 * BAND), lambda s, b, r, i: (s, 0, 0, 0)),
        ],
        out_specs=[cur_blk, cur_blk],
        out_shape=[out, out],
        compiler_params=_cparams(("parallel", "parallel", "parallel", "parallel")),
        name=f"attn_prompt_g{group}",
    )(qa, ka, ka, va, va, tbl)


def _attn_sample_kernel(cache_ref, q_ref, kn_ref, vn_ref, tblc_ref, tbln_ref, newc_ref, o_ref, lse_ref, *, wc, t, bb):
    kw = ATT_OUT_W
    ncol = wc // LANES
    lane = lax.broadcasted_iota(I32, (1, LANES), 1)
    keep_lanes = lane < LANES - t
    rows = ATT_HPG * t
    rhead = lax.broadcasted_iota(I32, (rows, kw), 0) // t
    lhead = lax.broadcasted_iota(I32, (rows, kw), 1) // ATT_DH
    hm = rhead == lhead
    for b in range(bb):
        tok = slice(b * t, (b + 1) * t)
        kn = jnp.concatenate([kn_ref[0, tok], kn_ref[1, tok]], axis=1)
        vn = jnp.concatenate([vn_ref[0, tok], vn_ref[1, tok]], axis=1)
        pad = jnp.zeros((BAND - t, kw), F32)
        knp = jnp.concatenate([kn, pad], axis=0)
        vnp = jnp.concatenate([vn, pad], axis=0)

        new_t = jnp.concatenate([knp.T, vnp.T], axis=0)
        tail = pltpu.roll(new_t, LANES - t, axis=1)
        prev_rot = None
        for c in range(ncol):
            rot = pltpu.roll(cache_ref[b, :, c * LANES:(c + 1) * LANES], LANES - t, axis=1)
            if c > 0:
                newc_ref[b, :, (c - 1) * LANES:c * LANES] = jnp.where(keep_lanes, prev_rot, rot)
            prev_rot = rot
        newc_ref[b, :, (ncol - 1) * LANES:ncol * LANES] = jnp.where(keep_lanes, prev_rot, tail)

        q = jnp.concatenate([q_ref[0, tok], q_ref[1, tok]], axis=1)
        qb = jnp.where(hm, jnp.concatenate([q] * ATT_HPG, axis=0), 0.0).astype(BF16)
        kt = cache_ref[b, 0:kw, :].astype(BF16)
        vt = cache_ref[b, kw:2 * kw, :].astype(BF16)
        s_c = _dot(qb, kt) + tblc_ref[...]
        s_n = _dot_nt(qb, knp.astype(BF16)) + tbln_ref[...]
        m = jnp.maximum(jnp.max(s_c, axis=-1, keepdims=True), jnp.max(s_n, axis=-1, keepdims=True))
        e_c = jnp.exp(s_c - m)
        e_n = jnp.exp(s_n - m)
        den = jnp.sum(e_c, axis=-1, keepdims=True) + jnp.sum(e_n, axis=-1, keepdims=True)
        o = _dot_nt((e_c / den).astype(BF16), vt) + _dot((e_n / den).astype(BF16), vnp.astype(BF16))
        lse = m + jnp.log(den)
        o = jnp.where(hm, o, 0.0)
        lse = jnp.where(hm, lse, 0.0)
        o_sel = o[0:t]
        lse_sel = lse[0:t]
        for h in range(1, ATT_HPG):
            o_sel = o_sel + o[h * t:(h + 1) * t]
            lse_sel = lse_sel + lse[h * t:(h + 1) * t]
        for s in range(SLABS_PER_GROUP):
            o_ref[s, tok, :] = o_sel[:, s * LANES:(s + 1) * LANES]
            lse_ref[s, tok, :] = lse_sel[:, s * LANES:(s + 1) * LANES]


def _attn_sample(qa, ka, va, cache, group, batch, t):
    win, dil = ATT_GROUPS[group]
    n_back = win // dil
    wc = cache.shape[1]
    assert wc == win and wc % LANES == 0 and t % SUBLANES == 0 and t <= BAND
    kw = ATT_OUT_W
    bb = max(1, min(ATTN_SAMPLE_BLOCK_BYTES // (2 * kw * wc * 4), ATTN_SAMPLE_MAX_BB, batch))
    assert batch % bb == 0
    slopes = _alibi_slopes()[group * ATT_HPG:(group + 1) * ATT_HPG]
    qi = jnp.arange(t)[:, None]
    idx = jnp.concatenate([jnp.arange(wc), wc + jnp.arange(BAND)])[None, :]
    dist = wc + qi - idx
    valid = (dist >= 0) & (dist % dil == 0) & (dist <= dil * n_back) & (idx < wc + t)
    tbl = jnp.where(valid[None], -slopes[:, None, None] * dist.astype(F32)[None], NEG_INF)
    tbl = tbl.reshape(ATT_HPG * t, wc + BAND)
    tblc, tbln = tbl[:, :wc], tbl[:, wc:]

    s0 = group
    new_spec = pl.BlockSpec((SLABS_PER_GROUP, bb * t, LANES), lambda b: (s0, b, 0))
    out_spec = pl.BlockSpec((SLABS_PER_GROUP, bb * t, LANES), lambda b: (0, b, 0))
    cache_spec = pl.BlockSpec((bb, 2 * kw, wc), lambda b: (b, 0, 0))
    out = jax.ShapeDtypeStruct((SLABS_PER_GROUP, batch * t, LANES), F32)
    cache_t = jnp.transpose(cache, (0, 2, 3, 4, 1)).reshape(batch, 2 * kw, wc)
    newc, o, lse = pl.pallas_call(
        functools.partial(_attn_sample_kernel, wc=wc, t=t, bb=bb),
        grid=(batch // bb,),
        in_specs=[
            cache_spec, new_spec, new_spec, new_spec,
            pl.BlockSpec(tblc.shape, lambda b: (0, 0)),
            pl.BlockSpec(tbln.shape, lambda b: (0, 0)),
        ],
        out_specs=[cache_spec, out_spec, out_spec],
        out_shape=[jax.ShapeDtypeStruct((batch, 2 * kw, wc), F32), out, out],
        compiler_params=_cparams(("parallel",)),
        name=f"attn_sample_g{group}",
    )(cache_t, qa, ka, va, tblc, tbln)
    newc = jnp.transpose(newc.reshape(batch, 2, ATT_HPG, ATT_DH, wc), (0, 4, 1, 2, 3))
    return o, lse, newc


def _merge_kernel(x_ref, oret_ref, gate_ref, o0_ref, o1_ref, o2_ref, l0_ref, l1_ref, l2_ref,
                  wa_ref, wb_ref, wo_ref, x1_ref, *scratch, dilated, tm):
    def rows(ref, g, s, kind):
        dil = ATT_GROUPS[g][1]
        if not dilated or dil == 1:
            return ref[s]
        scr = scratch[0]
        slot = ((g - 1) * SLABS_PER_GROUP + s) * 2 + kind
        for r in range(dil):
            scr[slot, pl.ds(r, tm // dil, stride=dil), :] = ref[s, :, r * LANES:(r + 1) * LANES]
        return scr[slot]

    slabs = []
    for s in range(SLABS_PER_GROUP):
        ls = [rows(ref, g, s, 0) for g, ref in enumerate((l0_ref, l1_ref, l2_ref))]
        os_ = [rows(ref, g, s, 1) for g, ref in enumerate((o0_ref, o1_ref, o2_ref))]
        m = jnp.maximum(jnp.maximum(ls[0], ls[1]), ls[2])
        es = [jnp.exp(l - m) for l in ls]
        den = es[0] + es[1] + es[2]
        slabs.append((es[0] / den) * os_[0] + (es[1] / den) * os_[1] + (es[2] / den) * os_[2])
    o_att = jnp.concatenate(slabs, axis=1).astype(BF16)
    ga = gate_ref[:, 0:D_MODEL]
    gb = gate_ref[:, D_MODEL:GATE_W]
    merged = _sigmoid(ga) * _dot(oret_ref[...], wa_ref[...]) + _sigmoid(gb) * _dot(o_att, wb_ref[...])
    x1_ref[...] = x_ref[...] + _dot(merged.astype(BF16), wo_ref[...])


def _merge(x, oret, zgate, os_, ls, wa, wb, wo, dilated):
    n = x.shape[0]
    tm = min(TM_MERGE, n)
    row = lambda w: pl.BlockSpec((tm, w), lambda i: (i, 0))
    full = lambda a: pl.BlockSpec(a.shape, lambda i: (0, 0))

    def slab(g):
        dil = ATT_GROUPS[g][1] if dilated else 1
        return pl.BlockSpec((SLABS_PER_GROUP, tm // dil, dil * LANES), lambda i: (0, i, 0))

    slabs = [slab(g) for g in range(len(ATT_GROUPS))]
    n_scr = (len(ATT_GROUPS) - 1) * SLABS_PER_GROUP * 2
    return pl.pallas_call(
        functools.partial(_merge_kernel, dilated=dilated, tm=tm),
        grid=(n // tm,),
        in_specs=[row(D_MODEL), row(RET_V_W), row(GATE_W)] + slabs + slabs + [full(wa), full(wb), full(wo)],
        out_specs=row(D_MODEL),
        out_shape=jax.ShapeDtypeStruct((n, D_MODEL), F32),
        scratch_shapes=[pltpu.VMEM((n_scr, tm, LANES), F32)] if dilated else [],
        compiler_params=_cparams(("parallel",)),
        name="merge_dilated" if dilated else "merge",
    )(x, oret, zgate, *os_, *ls, wa, wb, wo)


def _router_kernel(xp_ref, xs_ref, g_ref, wr_ref, br_ref, tri_ref, h_ref, idx_ref, rank_ref, w_ref, cnt_ref,
                   run_scr, *, n_prompt_tiles, tm):
    i = pl.program_id(0)

    @pl.when(i == 0)
    def _():
        run_scr[...] = jnp.zeros_like(run_scr)

    x = jnp.where(i < n_prompt_tiles, xp_ref[...], xs_ref[...])
    h = _rms(x, g_ref[...])
    for j in range(ROW_TILES):
        h_ref[pl.ds(j, tm, stride=ROW_TILES), :] = h[:, j * LANES:(j + 1) * LANES]

    logits = jnp.dot(h, wr_ref[...], precision=lax.Precision.HIGHEST, preferred_element_type=F32) + br_ref[...]
    lanes = lax.broadcasted_iota(I32, (1, LANES), 1)
    lanes_f = lanes.astype(F32)
    work = logits
    vals, idxs, hots = [], [], []
    for _ in range(TOP_K):
        m = jnp.max(work, axis=-1, keepdims=True)
        idx = jnp.min(jnp.where(work == m, lanes_f, float(LANES)), axis=-1, keepdims=True)
        hot = lanes_f == idx
        vals.append(m)
        idxs.append(idx)
        hots.append(hot)
        work = jnp.where(hot, -jnp.inf, work)
    exps = [jnp.exp(v - vals[0]) for v in vals]
    den = exps[0] + exps[1] + exps[2] + exps[3]

    cat = jnp.concatenate([hot.astype(BF16) for hot in hots], axis=1)
    cum = _dot(tri_ref[...], cat)
    prev = run_scr[...]
    idx_out = jnp.zeros((tm, LANES), I32)
    rank_out = jnp.zeros((tm, LANES), I32)
    w_out = jnp.zeros((tm, LANES), F32)
    for k in range(TOP_K):
        hot_f = hots[k].astype(F32)
        rank = jnp.sum(hot_f * (cum[:, k * LANES:(k + 1) * LANES] + prev), axis=-1, keepdims=True)
        prev = prev + jnp.sum(hot_f, axis=0, keepdims=True)
        idx_out = jnp.where(lanes == k, idxs[k].astype(I32), idx_out)
        rank_out = jnp.where(lanes == k, rank.astype(I32), rank_out)
        w_out = jnp.where(lanes == k, exps[k] / den, w_out)
    run_scr[...] = prev
    idx_ref[...] = idx_out
    rank_ref[...] = rank_out
    w_ref[...] = w_out
    cnt_ref[...] = prev.astype(I32)


def _router(x1p, x1s, g_moe, wr_pad, br_pad):
    tm = TM_ROUTE
    npt, nst = x1p.shape[0] // tm, x1s.shape[0] // tm
    n = x1p.shape[0] + x1s.shape[0]
    tri = (jnp.arange(tm)[:, None] > jnp.arange(tm)[None, :]).astype(BF16)
    row128 = pl.BlockSpec((tm, LANES), lambda i: (i, 0))
    return pl.pallas_call(
        functools.partial(_router_kernel, n_prompt_tiles=npt, tm=tm),
        grid=(npt + nst,),
        in_specs=[
            pl.BlockSpec((tm, D_MODEL), lambda i: (jnp.minimum(i, npt - 1), 0)),
            pl.BlockSpec((tm, D_MODEL), lambda i: (jnp.clip(i - npt, 0, nst - 1), 0)),
            pl.BlockSpec((1, D_MODEL), lambda i: (0, 0)),
            pl.BlockSpec((D_MODEL, LANES), lambda i: (0, 0)),
            pl.BlockSpec((1, LANES), lambda i: (0, 0)),
            pl.BlockSpec((tm, tm), lambda i: (0, 0)),
        ],
        out_specs=[
            pl.BlockSpec((tm * ROW_TILES, LANES), lambda i: (i, 0)),
            row128, row128, row128,
            pl.BlockSpec((1, LANES), lambda i: (0, 0)),
        ],
        out_shape=[
            jax.ShapeDtypeStruct((n * ROW_TILES, LANES), F32),
            jax.ShapeDtypeStruct((n, LANES), I32),
            jax.ShapeDtypeStruct((n, LANES), I32),
            jax.ShapeDtypeStruct((n, LANES), F32),
            jax.ShapeDtypeStruct((1, LANES), I32),
        ],
        scratch_shapes=[pltpu.VMEM((1, LANES), F32)],
        compiler_params=_cparams(("arbitrary",)),
        name="router",
    )(x1p, x1s, g_moe, wr_pad, br_pad, tri)


def _row_copy(src_ref, src_row, dst_ref, dst_row, sem):
    return pltpu.make_async_copy(
        src_ref.at[pl.ds(pl.multiple_of(src_row * ROW_TILES, ROW_TILES), ROW_TILES), :],
        dst_ref.at[pl.ds(pl.multiple_of(dst_row * ROW_TILES, ROW_TILES), ROW_TILES), :],
        sem)


def _dispatch_kernel(zstart_ref, zlen_ref, nused_ref, slots_ref, h_ref, hs_ref, zero_scr, sem, *, tm, n_tiles):
    i = pl.program_id(0)

    def zero_copy(row, n_rows):
        start = pl.multiple_of(row * ROW_TILES, ROW_TILES)
        return pltpu.make_async_copy(zero_scr.at[pl.ds(0, n_rows * ROW_TILES), :],
                                     hs_ref.at[pl.ds(start, n_rows * ROW_TILES), :], sem)

    def pad_copies(e, op):
        row, left = zstart_ref[e], zlen_ref[e]
        for b in reversed(range(TM_MOE.bit_length() - 1)):
            bit = (left >> b) & 1

            @pl.when(bit == 1)
            def _():
                op(zero_copy(row, 1 << b))

            row = row + (bit << b)

    def all_copies(op):
        lax.fori_loop(0, N_EXPERTS, lambda e, c: (pad_copies(e, op), c)[1], 0)
        lax.fori_loop(nused_ref[0], n_tiles, lambda tt, c: (op(zero_copy(tt * TM_MOE, TM_MOE)), c)[1], 0)

    @pl.when(i == 0)
    def _():
        zero_scr[...] = jnp.zeros_like(zero_scr)
        all_copies(lambda c: c.start())
        all_copies(lambda c: c.wait())

    def body(tok, carry):
        for k in range(TOP_K):
            _row_copy(h_ref, tok, hs_ref, slots_ref[0, 0, tok * TOP_K + k], sem).start(priority=k % 2)
        return carry

    lax.fori_loop(0, tm, body, 0)
    for _ in range(TOP_K):
        pltpu.make_async_copy(h_ref, hs_ref.at[pl.ds(0, tm * ROW_TILES), :], sem).wait()


def _dispatch(hrow, slots, zstart, zlen, n_used, n_tiles):
    tm = TM_ROUTE
    n = hrow.shape[0] // ROW_TILES
    nt = n // tm
    slots3 = slots.reshape(nt, 1, tm * TOP_K)
    grid_spec = pltpu.PrefetchScalarGridSpec(
        num_scalar_prefetch=3,
        grid=(nt,),
        in_specs=[
            pl.BlockSpec((1, 1, tm * TOP_K), lambda i, z, l, u: (i, 0, 0), memory_space=pltpu.SMEM),
            pl.BlockSpec((tm * ROW_TILES, LANES), lambda i, z, l, u: (i, 0)),
        ],
        out_specs=pl.BlockSpec(memory_space=pl.ANY),
        scratch_shapes=[pltpu.VMEM((TM_MOE * ROW_TILES, LANES), F32), pltpu.SemaphoreType.DMA(())],
    )
    return pl.pallas_call(
        functools.partial(_dispatch_kernel, tm=tm, n_tiles=n_tiles),
        grid_spec=grid_spec,
        out_shape=jax.ShapeDtypeStruct((n_tiles * TM_MOE * ROW_TILES, LANES), F32),
        compiler_params=_cparams(("arbitrary",)),
        name="dispatch",
    )(zstart, zlen, n_used, slots3, hrow)


def _experts_kernel(texp_ref, nused_ref, hs_ref, wu_ref, bu_ref, wd_ref, bd_ref, ys_ref, wu_scr, wd_scr):
    j = pl.program_id(0)
    tm = TM_MOE
    n_used = nused_ref[0]
    jj = jnp.minimum(j, n_used - 1)
    new_expert = (j == 0) | (texp_ref[jj] != texp_ref[jnp.maximum(jj - 1, 0)])

    @pl.when((j < n_used) & new_expert)
    def _():
        wu_scr[...] = wu_ref[0].astype(BF16)
        wd_scr[...] = wd_ref[0].astype(BF16)

    @pl.when(j < n_used)
    def _():
        x = jnp.concatenate(
            [hs_ref[pl.ds(c, tm, stride=ROW_TILES), :] for c in range(ROW_TILES)], axis=1).astype(BF16)
        u = _dot(x, wu_scr[...]) + bu_ref[0]
        glu = jnp.minimum(u[:, :D_EXPERT], SWIGLU_LIMIT)
        lin = jnp.clip(u[:, D_EXPERT:], -SWIGLU_LIMIT, SWIGLU_LIMIT)
        a = glu * _sigmoid(SWIGLU_ALPHA * glu) * (lin + 1.0)
        y = _dot(a.astype(BF16), wd_scr[...]) + bd_ref[0]
        for c in range(ROW_TILES):
            ys_ref[pl.ds(c, tm, stride=ROW_TILES), :] = y[:, c * LANES:(c + 1) * LANES]

    @pl.when(j >= n_used)
    def _():
        ys_ref[...] = jnp.zeros_like(ys_ref)


def _experts(hs, tile_expert, n_used, w_up, b_up, w_down, b_down, n_tiles):
    tm = TM_MOE
    tile = lambda j, te, nu: (jnp.minimum(j, nu[0] - 1), 0)
    exp3 = lambda j, te, nu: (te[jnp.minimum(j, nu[0] - 1)], 0, 0)
    grid_spec = pltpu.PrefetchScalarGridSpec(
        num_scalar_prefetch=2,
        grid=(n_tiles,),
        in_specs=[
            pl.BlockSpec((tm * ROW_TILES, LANES), tile),
            pl.BlockSpec((1, D_MODEL, 2 * D_EXPERT), exp3),
            pl.BlockSpec((1, 1, 2 * D_EXPERT), exp3),
            pl.BlockSpec((1, D_EXPERT, D_MODEL), exp3),
            pl.BlockSpec((1, 1, D_MODEL), exp3),
        ],
        out_specs=pl.BlockSpec((tm * ROW_TILES, LANES), lambda j, te, nu: (j, 0)),
        scratch_shapes=[pltpu.VMEM((D_MODEL, 2 * D_EXPERT), BF16), pltpu.VMEM((D_EXPERT, D_MODEL), BF16)],
    )
    return pl.pallas_call(
        _experts_kernel,
        grid_spec=grid_spec,
        out_shape=jax.ShapeDtypeStruct((n_tiles * tm * ROW_TILES, LANES), F32),
        compiler_params=_cparams(("arbitrary",)),
        name="experts",
    )(tile_expert, n_used, hs, w_up, b_up.reshape(N_EXPERTS, 1, -1), w_down, b_down.reshape(N_EXPERTS, 1, -1))


def _combine_kernel(slots_ref, x1_ref, w_ref, p_ref, g_ref, wpg_ref, wpe_ref, ys_ref, y_ref, buf, sem, *, tm):
    def body(tok, carry):
        for k in range(TOP_K):
            _row_copy(ys_ref, slots_ref[0, 0, tok * TOP_K + k], buf.at[k], tok, sem).start(priority=k % 2)
        return carry

    lax.fori_loop(0, tm, body, 0)
    for k in range(TOP_K):
        pltpu.make_async_copy(ys_ref.at[pl.ds(0, tm * ROW_TILES), :], buf.at[k], sem).wait()

    w = w_ref[...]
    wk = [jnp.broadcast_to(w[:, k:k + 1], (tm, LANES)) for k in range(TOP_K)]
    cols = []
    for c in range(ROW_TILES):
        acc = jnp.zeros((tm, LANES), F32)
        for k in range(TOP_K):
            acc = acc + wk[k] * buf[k, pl.ds(c, tm, stride=ROW_TILES), :]
        cols.append(acc)
    x2 = x1_ref[...] + jnp.concatenate(cols, axis=1)
    gate = _sigmoid(_dot(_rms(x2, g_ref[...]).astype(BF16), wpg_ref[...]))
    y_ref[...] = x2 + gate * _dot(p_ref[...].astype(BF16), wpe_ref[...])


def _combine(x1, slots, topw, p, g_ple, wpg, wpe, ys):
    n = x1.shape[0]
    tm = TM_COMB
    nt = n // tm
    slots3 = slots.reshape(nt, 1, tm * TOP_K)
    full = lambda a: pl.BlockSpec(a.shape, lambda i: (0, 0))
    return pl.pallas_call(
        functools.partial(_combine_kernel, tm=tm),
        grid=(nt,),
        in_specs=[
            pl.BlockSpec((1, 1, tm * TOP_K), lambda i: (i, 0, 0), memory_space=pltpu.SMEM),
            pl.BlockSpec((tm, D_MODEL), lambda i: (i, 0)),
            pl.BlockSpec((tm, LANES), lambda i: (i, 0)),
            pl.BlockSpec((tm, PLE_DIM), lambda i: (i, 0)),
            full(g_ple), full(wpg), full(wpe),
            pl.BlockSpec(memory_space=pl.ANY),
        ],
        out_specs=pl.BlockSpec((tm, D_MODEL), lambda i: (i, 0)),
        out_shape=jax.ShapeDtypeStruct((n, D_MODEL), F32),
        scratch_shapes=[pltpu.VMEM((TOP_K, tm * ROW_TILES, LANES), F32), pltpu.SemaphoreType.DMA(())],
        compiler_params=_cparams(("arbitrary",)),
        name="combine",
    )(slots3, x1, topw, p, g_ple, wpg, wpe, ys)


def _cache_from_slabs(ka, va, group, batch, seq, keep):
    dil = ATT_GROUPS[group][1]
    assert keep % dil == 0

    def tail(a):
        a = a.reshape(SLABS_PER_GROUP, batch, seq // dil, dil * LANES)[:, :, (seq - keep) // dil:]
        a = a.reshape(SLABS_PER_GROUP, batch, keep // dil, dil, LANES)
        return jnp.transpose(a, (1, 2, 3, 0, 4)).reshape(batch, keep, ATT_HPG, ATT_DH)

    return jnp.stack([tail(ka), tail(va)], axis=2)


def _layer(xp, xs, pp, ps, state, caches, g_mix, w_in, ret_gn, qn_g, kn_g, w_a, w_b, w_o, g_moe, w_router, b_router,
           w_up, b_up, w_down, b_down, g_ple, w_ple_gate, w_ple):
    batch, seq, _ = xp.shape
    dbatch, dseq, _ = xs.shape
    n_p, n_s = batch * seq, dbatch * dseq
    n = n_p + n_s

    w_in_bf = w_in.astype(BF16)
    g_mix2 = g_mix.reshape(1, D_MODEL)
    qg = qn_g.reshape(N_SLABS, 1, LANES)
    kg = kn_g.reshape(N_SLABS, 1, LANES)
    gn = ret_gn.reshape(RET_HEADS, 1, RET_DV)
    wa, wb, wo = w_a.astype(BF16), w_b.astype(BF16), w_o.astype(BF16)

    xp2 = xp.reshape(n_p, D_MODEL)
    xs2 = xs.reshape(n_s, D_MODEL)
    zret_p, zgate_p, *qkv_p = _inproj(xp2, g_mix2, w_in_bf, qg, kg, dilated=True)
    zret_s, zgate_s, qa_s, ka_s, va_s = _inproj(xs2, g_mix2, w_in_bf, qg, kg, dilated=False)

    oret_p, state_p = _ret_prompt(zret_p, gn, batch, seq)
    oret_s, state_s = _ret_sample(zret_s, state, gn, dbatch, dseq)

    o_p, l_p, o_s, l_s, win_p, win_s = [], [], [], [], [], []
    for g, (win, _) in enumerate(ATT_GROUPS):
        qa_g, ka_g, va_g = qkv_p[3 * g:3 * g + 3]
        o, l = _attn_prompt(qa_g, ka_g, va_g, g, batch, seq)
        o_p.append(o)
        l_p.append(l)
        win_p.append(_cache_from_slabs(ka_g, va_g, g, batch, seq, min(win, seq)))
        o, l, newc = _attn_sample(qa_s, ka_s, va_s, caches[g], g, dbatch, dseq)
        o_s.append(o)
        l_s.append(l)
        win_s.append(newc)

    x1p = _merge(xp2, oret_p, zgate_p, o_p, l_p, wa, wb, wo, dilated=True)
    x1s = _merge(xs2, oret_s, zgate_s, o_s, l_s, wa, wb, wo, dilated=False)

    wr_pad = jnp.zeros((D_MODEL, LANES), F32).at[:, :N_EXPERTS].set(w_router)
    br_pad = jnp.full((1, LANES), NEG_INF, F32).at[0, :N_EXPERTS].set(b_router)
    hrow, top_i, rank, top_w, counts = _router(x1p, x1s, g_moe.reshape(1, D_MODEL), wr_pad, br_pad)

    cnt = counts[0, :N_EXPERTS]
    padded = (cnt + TM_MOE - 1) // TM_MOE * TM_MOE
    ends = jnp.cumsum(padded)
    goff = ends - padded
    slots = goff[top_i[:, :TOP_K]] + rank[:, :TOP_K]
    n_tiles = (n * TOP_K) // TM_MOE + N_EXPERTS
    tile_ends = ends // TM_MOE
    n_used = tile_ends[-1:].astype(I32)
    tile_expert = jnp.sum(tile_ends[None, :] <= jnp.arange(n_tiles)[:, None], axis=1)
    tile_expert = jnp.minimum(tile_expert, N_EXPERTS - 1).astype(I32)
    zstart = (goff + cnt).astype(I32)
    zlen = (padded - cnt).astype(I32)

    hs = _dispatch(hrow, slots, zstart, zlen, n_used, n_tiles)
    ys = _experts(hs, tile_expert, n_used, w_up, b_up, w_down, b_down, n_tiles)

    g_ple2 = g_ple.reshape(1, D_MODEL)
    wpg, wpe = w_ple_gate.astype(BF16), w_ple.astype(BF16)
    yp = _combine(x1p, slots[:n_p], top_w[:n_p], pp.reshape(n_p, PLE_DIM), g_ple2, wpg, wpe, ys)
    ys_out = _combine(x1s, slots[n_p:], top_w[n_p:], ps.reshape(n_s, PLE_DIM), g_ple2, wpg, wpe, ys)
    return (yp.reshape(batch, seq, D_MODEL), ys_out.reshape(dbatch, dseq, D_MODEL), state_p, state_s, win_p, win_s)


def kernel(x_prompt, x_sample, p_prompt, p_sample, state_ret, cache_win128_kv, cache_win512_kv, cache_win2048_kv, norm_mix_g, w_in, ret_norm_g, q_norm_g, k_norm_g, w_a, w_b, w_o, norm_moe_g, w_router, b_router, w_up, b_up, w_down, b_down, norm_ple_g, w_ple_gate, w_ple):
    caches = (cache_win128_kv, cache_win512_kv, cache_win2048_kv)
    depth = w_in.shape[0]
    xp, xs = x_prompt, x_sample
    ret_p, ret_s = [], []
    win_p = [[] for _ in ATT_GROUPS]
    win_s = [[] for _ in ATT_GROUPS]
    for i in range(depth):
        xp, xs, sp, ss, bp, bs = _layer(
            xp, xs, p_prompt[i], p_sample[i], state_ret[i], [c[i] for c in caches], norm_mix_g[i], w_in[i],
            ret_norm_g[i], q_norm_g[i], k_norm_g[i], w_a[i], w_b[i], w_o[i], norm_moe_g[i], w_router[i], b_router[i],
            w_up[i], b_up[i], w_down[i], b_down[i], norm_ple_g[i], w_ple_gate[i], w_ple[i])
        ret_p.append(sp)
        ret_s.append(ss)
        for g in range(len(ATT_GROUPS)):
            win_p[g].append(bp[g])
            win_s[g].append(bs[g])
    return (xp, xs, jnp.stack(ret_p), jnp.stack(ret_s), jnp.stack(win_p[0]), jnp.stack(win_s[0]),
            jnp.stack(win_p[1]), jnp.stack(win_s[1]), jnp.stack(win_p[2]), jnp.stack(win_s[2]))
```

```python
import functools

import jax
import jax.numpy as jnp
from jax import lax
from jax.experimental import pallas as pl
from jax.experimental.pallas import tpu as pltpu

F32 = jnp.float32
BF16 = jnp.bfloat16
I32 = jnp.int32

D_MODEL = 1024
RET_HEADS = 4
RET_DK = 128
RET_DV = 256
RET_QK_W = RET_HEADS * RET_DK
RET_V_W = RET_HEADS * RET_DV
RET_CHUNK = 128
ATT_GROUPS = ((128, 1), (512, 4), (2048, 16))
ATT_HPG = 4
ATT_HEADS = ATT_HPG * len(ATT_GROUPS)
ATT_DH = 64
ATT_W = ATT_HEADS * ATT_DH
ATT_OUT_W = ATT_HPG * ATT_DH
BAND = 128
N_SLABS = ATT_W // 128
SLABS_PER_GROUP = ATT_OUT_W // 128
N_EXPERTS = 32
TOP_K = 4
D_EXPERT = 1024
SWIGLU_LIMIT = 7.0
SWIGLU_ALPHA = 1.702
PLE_DIM = 256
EPS = 1e-6
NEG_INF = -1e30

OFF_QR = 0
OFF_KR = OFF_QR + RET_QK_W
OFF_VR = OFF_KR + RET_QK_W
OFF_GR = OFF_VR + RET_V_W
OFF_QA = OFF_GR + RET_V_W
OFF_KA = OFF_QA + ATT_W
OFF_VA = OFF_KA + ATT_W
OFF_GA = OFF_VA + ATT_W
OFF_GB = OFF_GA + D_MODEL
N_IN = OFF_GB + D_MODEL
RET_W = OFF_QA
GATE_W = 2 * D_MODEL

LANES = 128
SUBLANES = 8
ROW_TILES = D_MODEL // LANES
VMEM_LIMIT = 56 * 1024 * 1024

TM_INPROJ = 256
TM_MERGE = 512
TM_ROUTE = 512
TM_MOE = 512
TM_MOE_SUB = 256
TM_COMB = 256
COMB_ROW_GROUP = 64
RET_CHUNKS_PER_STEP = 4
RET_SAMPLE_BB = 8
ATTN_Q_BLOCKS = 4
ATTN_SAMPLE_BLOCK_BYTES = 4 * 1024 * 1024
ATTN_SAMPLE_MAX_BB = 8


def _cparams(sem):
    return pltpu.CompilerParams(dimension_semantics=sem, vmem_limit_bytes=VMEM_LIMIT)


def _dot(a, b):
    return jnp.dot(a, b, preferred_element_type=F32)


def _dot_nt(a, b):
    return lax.dot_general(a, b, (((1,), (1,)), ((), ())), preferred_element_type=F32)


def _dot_tn(a, b):
    return lax.dot_general(a, b, (((0,), (0,)), ((), ())), preferred_element_type=F32)


def _rms(x, g):
    return x * lax.rsqrt(jnp.mean(x * x, axis=-1, keepdims=True) + EPS) * g


def _sigmoid(x):
    return 1.0 / (1.0 + jnp.exp(-x))


def _dilated_cols(dil):
    return dil * LANES


def _inproj_kernel(x_ref, g_ref, w_ref, qg_ref, kg_ref, zret_ref, zgate_ref, *rest, dilated, tm):
    h = _rms(x_ref[...], g_ref[...]).astype(BF16)

    def mm(lo, hi):
        return _dot(h, w_ref[:, lo:hi])

    zret_ref[:, OFF_QR:OFF_KR] = mm(OFF_QR, OFF_KR)
    zret_ref[:, OFF_KR:OFF_VR] = mm(OFF_KR, OFF_VR) * (RET_DK ** -0.5)
    zret_ref[:, OFF_VR:OFF_QA] = mm(OFF_VR, OFF_QA)
    zgate_ref[...] = mm(OFF_GA, N_IN)

    lane_lo = lax.broadcasted_iota(I32, (1, LANES), 1) < ATT_DH

    def headnorm(x, g):
        x2 = x * x
        lo = jnp.sum(jnp.where(lane_lo, x2, 0.0), axis=-1, keepdims=True)
        hi = jnp.sum(jnp.where(lane_lo, 0.0, x2), axis=-1, keepdims=True)
        ms = jnp.where(lane_lo, lo, hi) * (1.0 / ATT_DH)
        return x * lax.rsqrt(ms + EPS) * g

    q = mm(OFF_QA, OFF_KA)
    k = mm(OFF_KA, OFF_VA)
    v = mm(OFF_VA, OFF_GA)
    for s in range(N_SLABS):
        sl = slice(s * LANES, (s + 1) * LANES)
        vals = (headnorm(q[:, sl], qg_ref[s]) * (ATT_DH ** -0.5), headnorm(k[:, sl], kg_ref[s]), v[:, sl])
        g, sg = divmod(s, SLABS_PER_GROUP)
        dil = ATT_GROUPS[g][1]
        for a, val in enumerate(vals):
            if not dilated:
                rest[a][s] = val
            elif dil == 1:
                rest[3 * g + a][sg] = val
            else:
                scr = rest[-1]
                slot = (s - SLABS_PER_GROUP) * 3 + a
                scr[slot] = val
                for r in range(dil):
                    rest[3 * g + a][sg, :, r * LANES:(r + 1) * LANES] = scr[slot, pl.ds(r, tm // dil, stride=dil), :]


def _inproj(x, g_mix, w_in_bf, qg, kg, dilated):
    n = x.shape[0]
    tm = TM_INPROJ
    if dilated:
        slab_shapes, slab_specs = [], []
        for _, dil in ATT_GROUPS:
            assert tm % (dil * SUBLANES) == 0
            slab_shapes += [jax.ShapeDtypeStruct((SLABS_PER_GROUP, n // dil, _dilated_cols(dil)), F32)] * 3
            slab_specs += [pl.BlockSpec((SLABS_PER_GROUP, tm // dil, _dilated_cols(dil)), lambda i: (0, i, 0))] * 3
        scratch = [pltpu.VMEM(((N_SLABS - SLABS_PER_GROUP) * 3, tm, LANES), F32)]
    else:
        slab_shapes = [jax.ShapeDtypeStruct((N_SLABS, n, LANES), F32)] * 3
        slab_specs = [pl.BlockSpec((N_SLABS, tm, LANES), lambda i: (0, i, 0))] * 3
        scratch = []
    return pl.pallas_call(
        functools.partial(_inproj_kernel, dilated=dilated, tm=tm),
        grid=(n // tm,),
        in_specs=[
            pl.BlockSpec((tm, D_MODEL), lambda i: (i, 0)),
            pl.BlockSpec((1, D_MODEL), lambda i: (0, 0)),
            pl.BlockSpec((D_MODEL, N_IN), lambda i: (0, 0), pipeline_mode=pl.Buffered(1)),
            pl.BlockSpec((N_SLABS, 1, LANES), lambda i: (0, 0, 0)),
            pl.BlockSpec((N_SLABS, 1, LANES), lambda i: (0, 0, 0)),
        ],
        out_specs=[pl.BlockSpec((tm, RET_W), lambda i: (i, 0)), pl.BlockSpec((tm, GATE_W), lambda i: (i, 0))]
        + slab_specs,
        out_shape=[jax.ShapeDtypeStruct((n, RET_W), F32), jax.ShapeDtypeStruct((n, GATE_W), F32)] + slab_shapes,
        scratch_shapes=scratch,
        compiler_params=_cparams(("parallel",)),
        name="inproj_dilated" if dilated else "inproj",
    )(x, g_mix, w_in_bf, qg, kg)


def _ret_tables(chunk):
    lg = jnp.log1p(-jnp.exp2(-5.0 - jnp.arange(RET_HEADS, dtype=F32)))
    pos = jnp.arange(chunk, dtype=F32)
    rel = pos[:, None] - pos[None, :]
    intra = jnp.where(rel[None] >= 0, jnp.exp(lg[:, None, None] * jnp.maximum(rel, 0.0)[None]), 0.0)
    q_dec = jnp.exp(lg[:, None] * (pos[None, :] + 1.0))
    k_dec = jnp.exp(lg[:, None] * (chunk - 1.0 - pos[None, :]))
    c_dec = jnp.exp(lg * chunk)
    q_dec = jnp.broadcast_to(q_dec[:, :, None], (RET_HEADS, chunk, RET_DV))
    k_dec = jnp.broadcast_to(k_dec[:, :, None], (RET_HEADS, chunk, RET_DK))
    c_dec = jnp.broadcast_to(c_dec[:, None, None], (RET_HEADS, 1, RET_DV))
    return intra, q_dec, k_dec, c_dec


def _ret_head(q, kf, v, gr, s, intra, q_dec, k_dec, c_dec, gn):
    qb = q.astype(BF16)
    vb = v.astype(BF16)
    sc = _dot_nt(qb, kf.astype(BF16)) * intra
    o = _dot(sc.astype(BF16), vb) + _dot(qb, s.astype(BF16)) * q_dec
    s_new = s * c_dec + _dot_tn((kf * k_dec).astype(BF16), vb)
    y = _rms(o, gn) * (gr * _sigmoid(gr))
    return y, s_new


def _ret_prompt_kernel(q_ref, k_ref, v_ref, g_ref, intra_ref, qdec_ref, kdec_ref, cdec_ref, gn_ref,
                       o_ref, sfin_ref, s_scr):
    c_idx = pl.program_id(1)

    @pl.when(c_idx == 0)
    def _():
        s_scr[...] = jnp.zeros_like(s_scr)

    for c in range(RET_CHUNKS_PER_STEP):
        rows = slice(c * RET_CHUNK, (c + 1) * RET_CHUNK)
        for h in range(RET_HEADS):
            ks = slice(h * RET_DK, (h + 1) * RET_DK)
            vs = slice(h * RET_DV, (h + 1) * RET_DV)
            y, s_new = _ret_head(q_ref[rows, ks], k_ref[rows, ks], v_ref[rows, vs], g_ref[rows, vs], s_scr[h],
                                 intra_ref[h], qdec_ref[h], kdec_ref[h], cdec_ref[h], gn_ref[h])
            s_scr[h] = s_new
            o_ref[rows, vs] = y.astype(BF16)

    @pl.when(c_idx == pl.num_programs(1) - 1)
    def _():
        sfin_ref[0] = s_scr[...]


def _ret_prompt(zret, ret_gn, batch, seq):
    tc = RET_CHUNK * RET_CHUNKS_PER_STEP
    nc = seq // tc
    intra, q_dec, k_dec, c_dec = _ret_tables(RET_CHUNK)
    row = lambda b, c: b * nc + c
    const3 = lambda b, c: (0, 0, 0)
    return pl.pallas_call(
        _ret_prompt_kernel,
        grid=(batch, nc),
        in_specs=[
            pl.BlockSpec((tc, RET_QK_W), lambda b, c: (row(b, c), OFF_QR // RET_QK_W)),
            pl.BlockSpec((tc, RET_QK_W), lambda b, c: (row(b, c), OFF_KR // RET_QK_W)),
            pl.BlockSpec((tc, RET_V_W), lambda b, c: (row(b, c), OFF_VR // RET_V_W)),
            pl.BlockSpec((tc, RET_V_W), lambda b, c: (row(b, c), OFF_GR // RET_V_W)),
            pl.BlockSpec(intra.shape, const3),
            pl.BlockSpec(q_dec.shape, const3),
            pl.BlockSpec(k_dec.shape, const3),
            pl.BlockSpec(c_dec.shape, const3),
            pl.BlockSpec((RET_HEADS, 1, RET_DV), const3),
        ],
        out_specs=[
            pl.BlockSpec((tc, RET_V_W), lambda b, c: (row(b, c), 0)),
            pl.BlockSpec((1, RET_HEADS, RET_DK, RET_DV), lambda b, c: (b, 0, 0, 0)),
        ],
        out_shape=[
            jax.ShapeDtypeStruct((batch * seq, RET_V_W), BF16),
            jax.ShapeDtypeStruct((batch, RET_HEADS, RET_DK, RET_DV), F32),
        ],
        scratch_shapes=[pltpu.VMEM((RET_HEADS, RET_DK, RET_DV), F32)],
        compiler_params=_cparams(("parallel", "arbitrary")),
        name="ret_prompt",
    )(zret, zret, zret, zret, intra, q_dec, k_dec, c_dec, ret_gn)


def _ret_sample_kernel(q_ref, k_ref, v_ref, g_ref, s_ref, intra_ref, qdec_ref, kdec_ref, cdec_ref, gn_ref,
                       o_ref, snew_ref, *, t):
    for b in range(RET_SAMPLE_BB):
        rows = slice(b * t, (b + 1) * t)
        for h in range(RET_HEADS):
            ks = slice(h * RET_DK, (h + 1) * RET_DK)
            vs = slice(h * RET_DV, (h + 1) * RET_DV)
            y, s_new = _ret_head(q_ref[rows, ks], k_ref[rows, ks], v_ref[rows, vs], g_ref[rows, vs], s_ref[b, h],
                                 intra_ref[h], qdec_ref[h], kdec_ref[h], cdec_ref[h], gn_ref[h])
            snew_ref[b, h] = s_new
            o_ref[rows, vs] = y.astype(BF16)


def _ret_sample(zret, state, ret_gn, batch, t):
    bb = RET_SAMPLE_BB
    tr = bb * t
    intra, q_dec, k_dec, c_dec = _ret_tables(t)
    const3 = lambda i: (0, 0, 0)
    st_spec = pl.BlockSpec((bb, RET_HEADS, RET_DK, RET_DV), lambda i: (i, 0, 0, 0))
    return pl.pallas_call(
        functools.partial(_ret_sample_kernel, t=t),
        grid=(batch // bb,),
        in_specs=[
            pl.BlockSpec((tr, RET_QK_W), lambda i: (i, OFF_QR // RET_QK_W)),
            pl.BlockSpec((tr, RET_QK_W), lambda i: (i, OFF_KR // RET_QK_W)),
            pl.BlockSpec((tr, RET_V_W), lambda i: (i, OFF_VR // RET_V_W)),
            pl.BlockSpec((tr, RET_V_W), lambda i: (i, OFF_GR // RET_V_W)),
            st_spec,
            pl.BlockSpec(intra.shape, const3),
            pl.BlockSpec(q_dec.shape, const3),
            pl.BlockSpec(k_dec.shape, const3),
            pl.BlockSpec(c_dec.shape, const3),
            pl.BlockSpec((RET_HEADS, 1, RET_DV), const3),
        ],
        out_specs=[pl.BlockSpec((tr, RET_V_W), lambda i: (i, 0)), st_spec],
        out_shape=[
            jax.ShapeDtypeStruct((batch * t, RET_V_W), BF16),
            jax.ShapeDtypeStruct((batch, RET_HEADS, RET_DK, RET_DV), F32),
        ],
        compiler_params=_cparams(("parallel",)),
        name="ret_sample",
    )(zret, zret, zret, zret, state, intra, q_dec, k_dec, c_dec, ret_gn)


def _alibi_slopes():
    return jnp.exp2(-8.0 * (jnp.arange(ATT_HEADS, dtype=F32) + 1.0) / ATT_HEADS)


def _softmax_parts(s):
    m = jnp.max(s, axis=-1, keepdims=True)
    e = jnp.exp(s - m)
    den = jnp.sum(e, axis=-1, keepdims=True)
    return e / den, m + jnp.log(den)


def _attn_prompt_kernel(q_ref, kp_ref, kc_ref, vp_ref, vc_ref, tbl_ref, o_ref, lse_ref, *, qb):
    i = pl.program_id(3)
    k_all = jnp.concatenate([kp_ref[0], kc_ref[0]], axis=0).astype(BF16)
    v_all = jnp.concatenate([vp_ref[0], vc_ref[0]], axis=0).astype(BF16)
    lane_lo = lax.broadcasted_iota(I32, (1, LANES), 1) < ATT_DH
    kcol = lax.broadcasted_iota(I32, (1, 2 * BAND), 1)
    first = jnp.where((i == 0) & (kcol < BAND), NEG_INF, 0.0)
    for j in range(qb):
        rows = slice(j * BAND, (j + 1) * BAND)
        q = q_ref[0, rows, :]
        k = k_all[j * BAND:(j + 2) * BAND]
        v = v_all[j * BAND:(j + 2) * BAND]
        outs, lses = [], []
        for hh in range(2):
            keep = lane_lo if hh == 0 else jnp.logical_not(lane_lo)
            qm = jnp.where(keep, q, 0.0).astype(BF16)
            s = _dot_nt(qm, k) + tbl_ref[0, hh]
            if j == 0:
                s = s + first
            p, lse = _softmax_parts(s)
            outs.append(_dot(p.astype(BF16), v))
            lses.append(lse)
        o_ref[0, rows, :] = jnp.where(lane_lo, outs[0], outs[1])
        lse_ref[0, rows, :] = jnp.where(lane_lo, lses[0], lses[1])


def _attn_prompt(qa, ka, va, group, batch, seq):
    win, dil = ATT_GROUPS[group]
    n_back = win // dil
    nb = seq // dil // BAND
    qb = min(ATTN_Q_BLOCKS, nb)
    nbq = nb // qb
    slopes = _alibi_slopes()[group * ATT_HPG:(group + 1) * ATT_HPG]
    qi = jnp.arange(BAND)[:, None]
    kc = jnp.arange(2 * BAND)[None, :]
    j = qi - kc + BAND
    valid = (j >= 0) & (j <= n_back)
    tbl = jnp.where(valid[None], -slopes[:, None, None] * (dil * j).astype(F32)[None], NEG_INF)
    tbl = tbl.reshape(SLABS_PER_GROUP, 2, BAND, 2 * BAND)

    cur = lambda s, b, r, i: (s, b * nbq + i, r)
    prev = lambda s, b, r, i: (s, b * nb + jnp.maximum(i * qb - 1, 0), r)
    cur_blk = pl.BlockSpec((1, qb * BAND, LANES), cur)
    prev_blk = pl.BlockSpec((1, BAND, LANES), prev)
    out = jax.ShapeDtypeStruct(qa.shape, F32)
    return pl.pallas_call(
        functools.partial(_attn_prompt_kernel, qb=qb),
        grid=(SLABS_PER_GROUP, batch, dil, nbq),
        in_specs=[
            cur_blk, prev_blk, cur_blk, prev_blk, cur_blk,
            pl.BlockSpec((1, 2, BAND, 2 * BAND), lambda s, b, r, i: (s, 0, 0, 0)),
        ],
        out_specs=[cur_blk, cur_blk],
        out_shape=[out, out],
        compiler_params=_cparams(("parallel", "parallel", "parallel", "parallel")),
        name=f"attn_prompt_g{group}",
    )(qa, ka, ka, va, va, tbl)


def _attn_sample_kernel(cache_ref, q_ref, kn_ref, vn_ref, tblc_ref, tbln_ref, newc_ref, o_ref, lse_ref, *, wc, t, bb):
    kw = ATT_OUT_W
    ncol = wc // LANES
    lane = lax.broadcasted_iota(I32, (1, LANES), 1)
    keep_lanes = lane < LANES - t
    rows = ATT_HPG * t
    rhead = lax.broadcasted_iota(I32, (rows, kw), 0) // t
    lhead = lax.broadcasted_iota(I32, (rows, kw), 1) // ATT_DH
    hm = rhead == lhead
    for b in range(bb):
        tok = slice(b * t, (b + 1) * t)
        kn = jnp.concatenate([kn_ref[0, tok], kn_ref[1, tok]], axis=1)
        vn = jnp.concatenate([vn_ref[0, tok], vn_ref[1, tok]], axis=1)
        pad = jnp.zeros((BAND - t, kw), F32)
        knp = jnp.concatenate([kn, pad], axis=0)
        vnp = jnp.concatenate([vn, pad], axis=0)

        new_t = jnp.concatenate([knp.T, vnp.T], axis=0)
        tail = pltpu.roll(new_t, LANES - t, axis=1)
        prev_rot = None
        for c in range(ncol):
            rot = pltpu.roll(cache_ref[b, :, c * LANES:(c + 1) * LANES], LANES - t, axis=1)
            if c > 0:
                newc_ref[b, :, (c - 1) * LANES:c * LANES] = jnp.where(keep_lanes, prev_rot, rot)
            prev_rot = rot
        newc_ref[b, :, (ncol - 1) * LANES:ncol * LANES] = jnp.where(keep_lanes, prev_rot, tail)

        q = jnp.concatenate([q_ref[0, tok], q_ref[1, tok]], axis=1)
        qb = jnp.where(hm, jnp.concatenate([q] * ATT_HPG, axis=0), 0.0).astype(BF16)
        kt = cache_ref[b, 0:kw, :].astype(BF16)
        vt = cache_ref[b, kw:2 * kw, :].astype(BF16)
        s_c = _dot(qb, kt) + tblc_ref[...]
        s_n = _dot_nt(qb, knp.astype(BF16)) + tbln_ref[...]
        m = jnp.maximum(jnp.max(s_c, axis=-1, keepdims=True), jnp.max(s_n, axis=-1, keepdims=True))
        e_c = jnp.exp(s_c - m)
        e_n = jnp.exp(s_n - m)
        den = jnp.sum(e_c, axis=-1, keepdims=True) + jnp.sum(e_n, axis=-1, keepdims=True)
        o = _dot_nt((e_c / den).astype(BF16), vt) + _dot((e_n / den).astype(BF16), vnp.astype(BF16))
        lse = m + jnp.log(den)
        o = jnp.where(hm, o, 0.0)
        lse = jnp.where(hm, lse, 0.0)
        o_sel = o[0:t]
        lse_sel = lse[0:t]
        for h in range(1, ATT_HPG):
            o_sel = o_sel + o[h * t:(h + 1) * t]
            lse_sel = lse_sel + lse[h * t:(h + 1) * t]
        for s in range(SLABS_PER_GROUP):
            o_ref[s, tok, :] = o_sel[:, s * LANES:(s + 1) * LANES]
            lse_ref[s, tok, :] = lse_sel[:, s * LANES:(s + 1) * LANES]


def _attn_sample(qa, ka, va, cache, group, batch, t):
    win, dil = ATT_GROUPS[group]
    n_back = win // dil
    wc = cache.shape[1]
    assert wc == win and wc % LANES == 0 and t % SUBLANES == 0 and t <= BAND
    kw = ATT_OUT_W
    bb = max(1, min(ATTN_SAMPLE_BLOCK_BYTES // (2 * kw * wc * 4), ATTN_SAMPLE_MAX_BB, batch))
    assert batch % bb == 0
    slopes = _alibi_slopes()[group * ATT_HPG:(group + 1) * ATT_HPG]
    qi = jnp.arange(t)[:, None]
    idx = jnp.concatenate([jnp.arange(wc), wc + jnp.arange(BAND)])[None, :]
    dist = wc + qi - idx
    valid = (dist >= 0) & (dist % dil == 0) & (dist <= dil * n_back) & (idx < wc + t)
    tbl = jnp.where(valid[None], -slopes[:, None, None] * dist.astype(F32)[None], NEG_INF)
    tbl = tbl.reshape(ATT_HPG * t, wc + BAND)
    tblc, tbln = tbl[:, :wc], tbl[:, wc:]

    s0 = group
    new_spec = pl.BlockSpec((SLABS_PER_GROUP, bb * t, LANES), lambda b: (s0, b, 0))
    out_spec = pl.BlockSpec((SLABS_PER_GROUP, bb * t, LANES), lambda b: (0, b, 0))
    cache_spec = pl.BlockSpec((bb, 2 * kw, wc), lambda b: (b, 0, 0))
    out = jax.ShapeDtypeStruct((SLABS_PER_GROUP, batch * t, LANES), F32)
    cache_t = jnp.transpose(cache, (0, 2, 3, 4, 1)).reshape(batch, 2 * kw, wc)
    newc, o, lse = pl.pallas_call(
        functools.partial(_attn_sample_kernel, wc=wc, t=t, bb=bb),
        grid=(batch // bb,),
        in_specs=[
            cache_spec, new_spec, new_spec, new_spec,
            pl.BlockSpec(tblc.shape, lambda b: (0, 0)),
            pl.BlockSpec(tbln.shape, lambda b: (0, 0)),
        ],
        out_specs=[cache_spec, out_spec, out_spec],
        out_shape=[jax.ShapeDtypeStruct((batch, 2 * kw, wc), F32), out, out],
        compiler_params=_cparams(("parallel",)),
        name=f"attn_sample_g{group}",
    )(cache_t, qa, ka, va, tblc, tbln)
    newc = jnp.transpose(newc.reshape(batch, 2, ATT_HPG, ATT_DH, wc), (0, 4, 1, 2, 3))
    return o, lse, newc


def _merge_kernel(x_ref, oret_ref, gate_ref, o0_ref, o1_ref, o2_ref, l0_ref, l1_ref, l2_ref,
                  wa_ref, wb_ref, wo_ref, x1_ref, *scratch, dilated, tm):
    def rows(ref, g, s, kind):
        dil = ATT_GROUPS[g][1]
        if not dilated or dil == 1:
            return ref[s]
        scr = scratch[0]
        slot = ((g - 1) * SLABS_PER_GROUP + s) * 2 + kind
        for r in range(dil):
            scr[slot, pl.ds(r, tm // dil, stride=dil), :] = ref[s, :, r * LANES:(r + 1) * LANES]
        return scr[slot]

    slabs = []
    for s in range(SLABS_PER_GROUP):
        ls = [rows(ref, g, s, 0) for g, ref in enumerate((l0_ref, l1_ref, l2_ref))]
        os_ = [rows(ref, g, s, 1) for g, ref in enumerate((o0_ref, o1_ref, o2_ref))]
        m = jnp.maximum(jnp.maximum(ls[0], ls[1]), ls[2])
        es = [jnp.exp(l - m) for l in ls]
        den = es[0] + es[1] + es[2]
        slabs.append((es[0] / den) * os_[0] + (es[1] / den) * os_[1] + (es[2] / den) * os_[2])
    o_att = jnp.concatenate(slabs, axis=1).astype(BF16)
    ga = gate_ref[:, 0:D_MODEL]
    gb = gate_ref[:, D_MODEL:GATE_W]
    merged = _sigmoid(ga) * _dot(oret_ref[...], wa_ref[...]) + _sigmoid(gb) * _dot(o_att, wb_ref[...])
    x1_ref[...] = x_ref[...] + _dot(merged.astype(BF16), wo_ref[...])


def _merge(x, oret, zgate, os_, ls, wa, wb, wo, dilated):
    n = x.shape[0]
    tm = min(TM_MERGE, n)
    row = lambda w: pl.BlockSpec((tm, w), lambda i: (i, 0))
    full = lambda a: pl.BlockSpec(a.shape, lambda i: (0, 0))

    def slab(g):
        dil = ATT_GROUPS[g][1] if dilated else 1
        return pl.BlockSpec((SLABS_PER_GROUP, tm // dil, dil * LANES), lambda i: (0, i, 0))

    slabs = [slab(g) for g in range(len(ATT_GROUPS))]
    n_scr = (len(ATT_GROUPS) - 1) * SLABS_PER_GROUP * 2
    return pl.pallas_call(
        functools.partial(_merge_kernel, dilated=dilated, tm=tm),
        grid=(n // tm,),
        in_specs=[row(D_MODEL), row(RET_V_W), row(GATE_W)] + slabs + slabs + [full(wa), full(wb), full(wo)],
        out_specs=row(D_MODEL),
        out_shape=jax.ShapeDtypeStruct((n, D_MODEL), F32),
        scratch_shapes=[pltpu.VMEM((n_scr, tm, LANES), F32)] if dilated else [],
        compiler_params=_cparams(("parallel",)),
        name="merge_dilated" if dilated else "merge",
    )(x, oret, zgate, *os_, *ls, wa, wb, wo)


def _router_kernel(xp_ref, xs_ref, g_ref, wr_ref, br_ref, tri_ref, h_ref, idx_ref, rank_ref, w_ref, cnt_ref,
                   run_scr, *, n_prompt_tiles, tm):
    i = pl.program_id(0)

    @pl.when(i == 0)
    def _():
        run_scr[...] = jnp.zeros_like(run_scr)

    x = jnp.where(i < n_prompt_tiles, xp_ref[...], xs_ref[...])
    h = _rms(x, g_ref[...])
    for j in range(ROW_TILES):
        h_ref[pl.ds(j, tm, stride=ROW_TILES), :] = h[:, j * LANES:(j + 1) * LANES]

    logits = jnp.dot(h, wr_ref[...], precision=lax.Precision.HIGHEST, preferred_element_type=F32) + br_ref[...]
    lanes = lax.broadcasted_iota(I32, (1, LANES), 1)
    lanes_f = lanes.astype(F32)
    work = logits
    vals, idxs, hots = [], [], []
    for _ in range(TOP_K):
        m = jnp.max(work, axis=-1, keepdims=True)
        idx = jnp.min(jnp.where(work == m, lanes_f, float(LANES)), axis=-1, keepdims=True)
        hot = lanes_f == idx
        vals.append(m)
        idxs.append(idx)
        hots.append(hot)
        work = jnp.where(hot, -jnp.inf, work)
    exps = [jnp.exp(v - vals[0]) for v in vals]
    den = exps[0] + exps[1] + exps[2] + exps[3]

    cat = jnp.concatenate([hot.astype(BF16) for hot in hots], axis=1)
    cum = _dot(tri_ref[...], cat)
    prev = run_scr[...]
    idx_out = jnp.zeros((tm, LANES), I32)
    rank_out = jnp.zeros((tm, LANES), I32)
    w_out = jnp.zeros((tm, LANES), F32)
    for k in range(TOP_K):
        hot_f = hots[k].astype(F32)
        rank = jnp.sum(hot_f * (cum[:, k * LANES:(k + 1) * LANES] + prev), axis=-1, keepdims=True)
        prev = prev + jnp.sum(hot_f, axis=0, keepdims=True)
        idx_out = jnp.where(lanes == k, idxs[k].astype(I32), idx_out)
        rank_out = jnp.where(lanes == k, rank.astype(I32), rank_out)
        w_out = jnp.where(lanes == k, exps[k] / den, w_out)
    run_scr[...] = prev
    idx_ref[...] = idx_out
    rank_ref[...] = rank_out
    w_ref[...] = w_out
    cnt_ref[...] = prev.astype(I32)


def _router(x1p, x1s, g_moe, wr_pad, br_pad):
    tm = TM_ROUTE
    npt, nst = x1p.shape[0] // tm, x1s.shape[0] // tm
    n = x1p.shape[0] + x1s.shape[0]
    tri = (jnp.arange(tm)[:, None] > jnp.arange(tm)[None, :]).astype(BF16)
    row128 = pl.BlockSpec((tm, LANES), lambda i: (i, 0))
    return pl.pallas_call(
        functools.partial(_router_kernel, n_prompt_tiles=npt, tm=tm),
        grid=(npt + nst,),
        in_specs=[
            pl.BlockSpec((tm, D_MODEL), lambda i: (jnp.minimum(i, npt - 1), 0)),
            pl.BlockSpec((tm, D_MODEL), lambda i: (jnp.clip(i - npt, 0, nst - 1), 0)),
            pl.BlockSpec((1, D_MODEL), lambda i: (0, 0)),
            pl.BlockSpec((D_MODEL, LANES), lambda i: (0, 0)),
            pl.BlockSpec((1, LANES), lambda i: (0, 0)),
            pl.BlockSpec((tm, tm), lambda i: (0, 0)),
        ],
        out_specs=[
            pl.BlockSpec((tm * ROW_TILES, LANES), lambda i: (i, 0)),
            row128, row128, row128,
            pl.BlockSpec((1, LANES), lambda i: (0, 0)),
        ],
        out_shape=[
            jax.ShapeDtypeStruct((n * ROW_TILES, LANES), F32),
            jax.ShapeDtypeStruct((n, LANES), I32),
            jax.ShapeDtypeStruct((n, LANES), I32),
            jax.ShapeDtypeStruct((n, LANES), F32),
            jax.ShapeDtypeStruct((1, LANES), I32),
        ],
        scratch_shapes=[pltpu.VMEM((1, LANES), F32)],
        compiler_params=_cparams(("arbitrary",)),
        name="router",
    )(x1p, x1s, g_moe, wr_pad, br_pad, tri)


def _row_copy(src_ref, src_row, dst_ref, dst_row, sem):
    return pltpu.make_async_copy(
        src_ref.at[pl.ds(pl.multiple_of(src_row * ROW_TILES, ROW_TILES), ROW_TILES), :],
        dst_ref.at[pl.ds(pl.multiple_of(dst_row * ROW_TILES, ROW_TILES), ROW_TILES), :],
        sem)


def _dispatch_kernel(zstart_ref, zlen_ref, nused_ref, slots_ref, h_ref, hs_ref, zero_scr, sem, *, tm, n_tiles):
    i = pl.program_id(0)

    def zero_copy(row, n_rows):
        start = pl.multiple_of(row * ROW_TILES, ROW_TILES)
        return pltpu.make_async_copy(zero_scr.at[pl.ds(0, n_rows * ROW_TILES), :],
                                     hs_ref.at[pl.ds(start, n_rows * ROW_TILES), :], sem)

    def pad_copies(e, op):
        row, left = zstart_ref[e], zlen_ref[e]
        for b in reversed(range(TM_MOE.bit_length() - 1)):
            bit = (left >> b) & 1

            @pl.when(bit == 1)
            def _():
                op(zero_copy(row, 1 << b))

            row = row + (bit << b)

    def all_copies(op):
        lax.fori_loop(0, N_EXPERTS, lambda e, c: (pad_copies(e, op), c)[1], 0)
        lax.fori_loop(nused_ref[0], n_tiles, lambda tt, c: (op(zero_copy(tt * TM_MOE, TM_MOE)), c)[1], 0)

    @pl.when(i == 0)
    def _():
        zero_scr[...] = jnp.zeros_like(zero_scr)
        all_copies(lambda c: c.start())
        all_copies(lambda c: c.wait())

    def body(tok, carry):
        for k in range(TOP_K):
            _row_copy(h_ref, tok, hs_ref, slots_ref[0, 0, tok * TOP_K + k], sem).start(priority=k % 2)
        return carry

    lax.fori_loop(0, tm, body, 0, unroll=8)
    for _ in range(TOP_K):
        pltpu.make_async_copy(h_ref, hs_ref.at[pl.ds(0, tm * ROW_TILES), :], sem).wait()


def _dispatch(hrow, slots, zstart, zlen, n_used, n_tiles):
    tm = TM_ROUTE
    n = hrow.shape[0] // ROW_TILES
    nt = n // tm
    slots3 = slots.reshape(nt, 1, tm * TOP_K)
    grid_spec = pltpu.PrefetchScalarGridSpec(
        num_scalar_prefetch=3,
        grid=(nt,),
        in_specs=[
            pl.BlockSpec((1, 1, tm * TOP_K), lambda i, z, l, u: (i, 0, 0), memory_space=pltpu.SMEM),
            pl.BlockSpec((tm * ROW_TILES, LANES), lambda i, z, l, u: (i, 0)),
        ],
        out_specs=pl.BlockSpec(memory_space=pl.ANY),
        scratch_shapes=[pltpu.VMEM((TM_MOE * ROW_TILES, LANES), F32), pltpu.SemaphoreType.DMA(())],
    )
    return pl.pallas_call(
        functools.partial(_dispatch_kernel, tm=tm, n_tiles=n_tiles),
        grid_spec=grid_spec,
        out_shape=jax.ShapeDtypeStruct((n_tiles * TM_MOE * ROW_TILES, LANES), F32),
        compiler_params=_cparams(("arbitrary",)),
        name="dispatch",
    )(zstart, zlen, n_used, slots3, hrow)


def _experts_kernel(texp_ref, nused_ref, hs_ref, wu_ref, bu_ref, wd_ref, bd_ref, ys_ref, wu_scr, wd_scr):
    j = pl.program_id(0)
    tm = TM_MOE
    n_used = nused_ref[0]
    jj = jnp.minimum(j, n_used - 1)
    new_expert = (j == 0) | (texp_ref[jj] != texp_ref[jnp.maximum(jj - 1, 0)])

    @pl.when((j < n_used) & new_expert)
    def _():
        wu_scr[...] = wu_ref[0].astype(BF16)
        wd_scr[...] = wd_ref[0].astype(BF16)

    @pl.when(j < n_used)
    def _():
        for sub in range(tm // TM_MOE_SUB):
            base = sub * TM_MOE_SUB * ROW_TILES
            x = jnp.concatenate(
                [hs_ref[pl.ds(base + c, TM_MOE_SUB, stride=ROW_TILES), :] for c in range(ROW_TILES)],
                axis=1).astype(BF16)
            u = _dot(x, wu_scr[...]) + bu_ref[0]
            glu = jnp.minimum(u[:, :D_EXPERT], SWIGLU_LIMIT)
            lin = jnp.clip(u[:, D_EXPERT:], -SWIGLU_LIMIT, SWIGLU_LIMIT)
            a = glu * _sigmoid(SWIGLU_ALPHA * glu) * (lin + 1.0)
            y = _dot(a.astype(BF16), wd_scr[...]) + bd_ref[0]
            for c in range(ROW_TILES):
                ys_ref[pl.ds(base + c, TM_MOE_SUB, stride=ROW_TILES), :] = y[:, c * LANES:(c + 1) * LANES]

    @pl.when(j >= n_used)
    def _():
        ys_ref[...] = jnp.zeros_like(ys_ref)


def _experts(hs, tile_expert, n_used, w_up, b_up, w_down, b_down, n_tiles):
    tm = TM_MOE
    tile = lambda j, te, nu: (jnp.minimum(j, nu[0] - 1), 0)
    exp3 = lambda j, te, nu: (te[jnp.minimum(j, nu[0] - 1)], 0, 0)
    grid_spec = pltpu.PrefetchScalarGridSpec(
        num_scalar_prefetch=2,
        grid=(n_tiles,),
        in_specs=[
            pl.BlockSpec((tm * ROW_TILES, LANES), tile),
            pl.BlockSpec((1, D_MODEL, 2 * D_EXPERT), exp3),
            pl.BlockSpec((1, 1, 2 * D_EXPERT), exp3),
            pl.BlockSpec((1, D_EXPERT, D_MODEL), exp3),
            pl.BlockSpec((1, 1, D_MODEL), exp3),
        ],
        out_specs=pl.BlockSpec((tm * ROW_TILES, LANES), lambda j, te, nu: (j, 0)),
        scratch_shapes=[pltpu.VMEM((D_MODEL, 2 * D_EXPERT), BF16), pltpu.VMEM((D_EXPERT, D_MODEL), BF16)],
    )
    return pl.pallas_call(
        _experts_kernel,
        grid_spec=grid_spec,
        out_shape=jax.ShapeDtypeStruct((n_tiles * tm * ROW_TILES, LANES), F32),
        compiler_params=_cparams(("arbitrary",)),
        name="experts",
    )(tile_expert, n_used, hs, w_up, b_up.reshape(N_EXPERTS, 1, -1), w_down, b_down.reshape(N_EXPERTS, 1, -1))


def _combine_kernel(slots_ref, nslots_ref, x1_ref, w_ref, p_ref, g_ref, wpg_ref, wpe_ref, ys_ref, y_ref,
                    buf, x2_scr, sems, *, tm):
    i = pl.program_id(0)
    cur = lax.rem(i, 2)
    nxt = 1 - cur

    def gather(sref, half, tok, k):
        return _row_copy(ys_ref, sref[0, 0, tok * TOP_K + k], buf.at[half, k], tok, sems.at[half])

    def wait_tile(half):
        for k in range(TOP_K):
            pltpu.make_async_copy(ys_ref.at[pl.ds(0, tm * ROW_TILES), :], buf.at[half, k], sems.at[half]).wait()

    @pl.when(i == 0)
    def _():
        def body(tok, carry):
            for k in range(TOP_K):
                gather(slots_ref, 0, tok, k).start(priority=k % 2)
            return carry

        lax.fori_loop(0, tm, body, 0)

    wait_tile(cur)
    rg = COMB_ROW_GROUP
    per = tm // (tm // rg * ROW_TILES)
    for r in range(tm // rg):
        w = w_ref[r * rg:(r + 1) * rg, :]
        wk = [jnp.broadcast_to(w[:, k:k + 1], (rg, LANES)) for k in range(TOP_K)]
        for c in range(ROW_TILES):
            acc = x1_ref[r * rg:(r + 1) * rg, c * LANES:(c + 1) * LANES]
            for k in range(TOP_K):
                acc = acc + wk[k] * buf[cur, k, pl.ds(r * rg * ROW_TILES + c, rg, stride=ROW_TILES), :]
            g = r * ROW_TILES + c
            for tok in range(g * per, (g + 1) * per):
                for k in range(TOP_K):
                    gather(nslots_ref, nxt, tok, k).start(priority=k % 2)
            x2_scr[r * rg:(r + 1) * rg, c * LANES:(c + 1) * LANES] = acc
    x2 = x2_scr[...]
    gate = _sigmoid(_dot(_rms(x2, g_ref[...]).astype(BF16), wpg_ref[...]))
    y_ref[...] = x2 + gate * _dot(p_ref[...].astype(BF16), wpe_ref[...])

    @pl.when(i == pl.num_programs(0) - 1)
    def _():
        wait_tile(nxt)


def _combine(x1, slots, topw, p, g_ple, wpg, wpe, ys):
    n = x1.shape[0]
    tm = TM_COMB
    nt = n // tm
    slots3 = slots.reshape(nt, 1, tm * TOP_K)
    full = lambda a: pl.BlockSpec(a.shape, lambda i: (0, 0))
    return pl.pallas_call(
        functools.partial(_combine_kernel, tm=tm),
        grid=(nt,),
        in_specs=[
            pl.BlockSpec((1, 1, tm * TOP_K), lambda i: (i, 0, 0), memory_space=pltpu.SMEM),
            pl.BlockSpec((1, 1, tm * TOP_K), lambda i: (jnp.minimum(i + 1, nt - 1), 0, 0), memory_space=pltpu.SMEM),
            pl.BlockSpec((tm, D_MODEL), lambda i: (i, 0)),
            pl.BlockSpec((tm, LANES), lambda i: (i, 0)),
            pl.BlockSpec((tm, PLE_DIM), lambda i: (i, 0)),
            full(g_ple), full(wpg), full(wpe),
            pl.BlockSpec(memory_space=pl.ANY),
        ],
        out_specs=pl.BlockSpec((tm, D_MODEL), lambda i: (i, 0)),
        out_shape=jax.ShapeDtypeStruct((n, D_MODEL), F32),
        scratch_shapes=[pltpu.VMEM((2, TOP_K, tm * ROW_TILES, LANES), F32), pltpu.VMEM((tm, D_MODEL), F32),
                        pltpu.SemaphoreType.DMA((2,))],
        compiler_params=_cparams(("arbitrary",)),
        name="combine",
    )(slots3, slots3, x1, topw, p, g_ple, wpg, wpe, ys)


def _cache_from_slabs(ka, va, group, batch, seq, keep):
    dil = ATT_GROUPS[group][1]
    assert keep % dil == 0

    def tail(a):
        a = a.reshape(SLABS_PER_GROUP, batch, seq // dil, dil * LANES)[:, :, (seq - keep) // dil:]
        a = a.reshape(SLABS_PER_GROUP, batch, keep // dil, dil, LANES)
        return jnp.transpose(a, (1, 2, 3, 0, 4)).reshape(batch, keep, ATT_HPG, ATT_DH)

    return jnp.stack([tail(ka), tail(va)], axis=2)


def _layer(xp, xs, pp, ps, state, caches, g_mix, w_in, ret_gn, qn_g, kn_g, w_a, w_b, w_o, g_moe, w_router, b_router,
           w_up, b_up, w_down, b_down, g_ple, w_ple_gate, w_ple):
    batch, seq, _ = xp.shape
    dbatch, dseq, _ = xs.shape
    n_p, n_s = batch * seq, dbatch * dseq
    n = n_p + n_s

    w_in_bf = w_in.astype(BF16)
    g_mix2 = g_mix.reshape(1, D_MODEL)
    qg = qn_g.reshape(N_SLABS, 1, LANES)
    kg = kn_g.reshape(N_SLABS, 1, LANES)
    gn = ret_gn.reshape(RET_HEADS, 1, RET_DV)
    wa, wb, wo = w_a.astype(BF16), w_b.astype(BF16), w_o.astype(BF16)

    xp2 = xp.reshape(n_p, D_MODEL)
    xs2 = xs.reshape(n_s, D_MODEL)
    zret_p, zgate_p, *qkv_p = _inproj(xp2, g_mix2, w_in_bf, qg, kg, dilated=True)
    zret_s, zgate_s, qa_s, ka_s, va_s = _inproj(xs2, g_mix2, w_in_bf, qg, kg, dilated=False)

    oret_p, state_p = _ret_prompt(zret_p, gn, batch, seq)
    oret_s, state_s = _ret_sample(zret_s, state, gn, dbatch, dseq)

    o_p, l_p, o_s, l_s, win_p, win_s = [], [], [], [], [], []
    for g, (win, _) in enumerate(ATT_GROUPS):
        qa_g, ka_g, va_g = qkv_p[3 * g:3 * g + 3]
        o, l = _attn_prompt(qa_g, ka_g, va_g, g, batch, seq)
        o_p.append(o)
        l_p.append(l)
        win_p.append(_cache_from_slabs(ka_g, va_g, g, batch, seq, min(win, seq)))
        o, l, newc = _attn_sample(qa_s, ka_s, va_s, caches[g], g, dbatch, dseq)
        o_s.append(o)
        l_s.append(l)
        win_s.append(newc)

    x1p = _merge(xp2, oret_p, zgate_p, o_p, l_p, wa, wb, wo, dilated=True)
    x1s = _merge(xs2, oret_s, zgate_s, o_s, l_s, wa, wb, wo, dilated=False)

    wr_pad = jnp.zeros((D_MODEL, LANES), F32).at[:, :N_EXPERTS].set(w_router)
    br_pad = jnp.full((1, LANES), NEG_INF, F32).at[0, :N_EXPERTS].set(b_router)
    hrow, top_i, rank, top_w, counts = _router(x1p, x1s, g_moe.reshape(1, D_MODEL), wr_pad, br_pad)

    cnt = counts[0, :N_EXPERTS]
    padded = (cnt + TM_MOE - 1) // TM_MOE * TM_MOE
    ends = jnp.cumsum(padded)
    goff = ends - padded
    slots = goff[top_i[:, :TOP_K]] + rank[:, :TOP_K]
    n_tiles = (n * TOP_K) // TM_MOE + N_EXPERTS
    tile_ends = ends // TM_MOE
    n_used = tile_ends[-1:].astype(I32)
    tile_expert = jnp.sum(tile_ends[None, :] <= jnp.arange(n_tiles)[:, None], axis=1)
    tile_expert = jnp.minimum(tile_expert, N_EXPERTS - 1).astype(I32)
    zstart = (goff + cnt).astype(I32)
    zlen = (padded - cnt).astype(I32)

    hs = _dispatch(hrow, slots, zstart, zlen, n_used, n_tiles)
    ys = _experts(hs, tile_expert, n_used, w_up, b_up, w_down, b_down, n_tiles)

    g_ple2 = g_ple.reshape(1, D_MODEL)
    wpg, wpe = w_ple_gate.astype(BF16), w_ple.astype(BF16)
    yp = _combine(x1p, slots[:n_p], top_w[:n_p], pp.reshape(n_p, PLE_DIM), g_ple2, wpg, wpe, ys)
    ys_out = _combine(x1s, slots[n_p:], top_w[n_p:], ps.reshape(n_s, PLE_DIM), g_ple2, wpg, wpe, ys)
    return (yp.reshape(batch, seq, D_MODEL), ys_out.reshape(dbatch, dseq, D_MODEL), state_p, state_s, win_p, win_s)


def kernel(x_prompt, x_sample, p_prompt, p_sample, state_ret, cache_win128_kv, cache_win512_kv, cache_win2048_kv, norm_mix_g, w_in, ret_norm_g, q_norm_g, k_norm_g, w_a, w_b, w_o, norm_moe_g, w_router, b_router, w_up, b_up, w_down, b_down, norm_ple_g, w_ple_gate, w_ple):
    caches = (cache_win128_kv, cache_win512_kv, cache_win2048_kv)
    depth = w_in.shape[0]
    xp, xs = x_prompt, x_sample
    ret_p, ret_s = [], []
    win_p = [[] for _ in ATT_GROUPS]
    win_s = [[] for _ in ATT_GROUPS]
    for i in range(depth):
        xp, xs, sp, ss, bp, bs = _layer(
            xp, xs, p_prompt[i], p_sample[i], state_ret[i], [c[i] for c in caches], norm_mix_g[i], w_in[i],
            ret_norm_g[i], q_norm_g[i], k_norm_g[i], w_a[i], w_b[i], w_o[i], norm_moe_g[i], w_router[i], b_router[i],
            w_up[i], b_up[i], w_down[i], b_down[i], norm_ple_g[i], w_ple_gate[i], w_ple[i])
        ret_p.append(sp)
        ret_s.append(ss)
        for g in range(len(ATT_GROUPS)):
            win_p[g].append(bp[g])
            win_s[g].append(bs[g])
    return (xp, xs, jnp.stack(ret_p), jnp.stack(ret_s), jnp.stack(win_p[0]), jnp.stack(win_s[0]),
            jnp.stack(win_p[1]), jnp.stack(win_s[1]), jnp.stack(win_p[2]), jnp.stack(win_s[2]))
```

```python
import functools

import jax
import jax.numpy as jnp
from jax import lax
from jax.experimental import pallas as pl
from jax.experimental.pallas import tpu as pltpu

F32 = jnp.float32
BF16 = jnp.bfloat16
I32 = jnp.int32

D_MODEL = 1024
RET_HEADS = 4
RET_DK = 128
RET_DV = 256
RET_QK_W = RET_HEADS * RET_DK
RET_V_W = RET_HEADS * RET_DV
RET_CHUNK = 128
ATT_GROUPS = ((128, 1), (512, 4), (2048, 16))
ATT_HPG = 4
ATT_HEADS = ATT_HPG * len(ATT_GROUPS)
ATT_DH = 64
ATT_W = ATT_HEADS * ATT_DH
ATT_OUT_W = ATT_HPG * ATT_DH
BAND = 128
N_SLABS = ATT_W // 128
SLABS_PER_GROUP = ATT_OUT_W // 128
N_EXPERTS = 32
TOP_K = 4
D_EXPERT = 1024
SWIGLU_LIMIT = 7.0
SWIGLU_ALPHA = 1.702
PLE_DIM = 256
EPS = 1e-6
NEG_INF = -1e30

OFF_QR = 0
OFF_KR = OFF_QR + RET_QK_W
OFF_VR = OFF_KR + RET_QK_W
OFF_GR = OFF_VR + RET_V_W
OFF_QA = OFF_GR + RET_V_W
OFF_KA = OFF_QA + ATT_W
OFF_VA = OFF_KA + ATT_W
OFF_GA = OFF_VA + ATT_W
OFF_GB = OFF_GA + D_MODEL
N_IN = OFF_GB + D_MODEL
RET_W = OFF_QA
GATE_W = 2 * D_MODEL

LANES = 128
SUBLANES = 8
ROW_TILES = D_MODEL // LANES
VMEM_LIMIT = 56 * 1024 * 1024

TM_INPROJ = 256
TM_MERGE = 512
TM_ROUTE = 512
TM_MOE = 512
TM_COMB = 256
COMB_ROW_GROUP = 64
RET_CHUNKS_PER_STEP = 8
RET_SAMPLE_BB = 8
ATTN_Q_BLOCKS = 8
ATTN_SAMPLE_BLOCK_BYTES = 8 * 1024 * 1024
ATTN_SAMPLE_MAX_BB = 16


def _cparams(sem):
    return pltpu.CompilerParams(dimension_semantics=sem, vmem_limit_bytes=VMEM_LIMIT)


def _dot(a, b):
    return jnp.dot(a, b, preferred_element_type=F32)


def _dot_nt(a, b):
    return lax.dot_general(a, b, (((1,), (1,)), ((), ())), preferred_element_type=F32)


def _dot_tn(a, b):
    return lax.dot_general(a, b, (((0,), (0,)), ((), ())), preferred_element_type=F32)


def _rms(x, g):
    return x * lax.rsqrt(jnp.mean(x * x, axis=-1, keepdims=True) + EPS) * g


def _sigmoid(x):
    return 1.0 / (1.0 + jnp.exp(-x))


def _dilated_cols(dil):
    return dil * LANES


def _inproj_kernel(x_ref, g_ref, w_ref, qg_ref, kg_ref, zret_ref, zgate_ref, *rest, dilated, tm):
    h = _rms(x_ref[...], g_ref[...]).astype(BF16)

    def mm(lo, hi):
        return _dot(h, w_ref[:, lo:hi])

    zret_ref[:, OFF_QR:OFF_KR] = mm(OFF_QR, OFF_KR)
    zret_ref[:, OFF_KR:OFF_VR] = mm(OFF_KR, OFF_VR) * (RET_DK ** -0.5)
    zret_ref[:, OFF_VR:OFF_QA] = mm(OFF_VR, OFF_QA)
    zgate_ref[...] = mm(OFF_GA, N_IN)

    lane_lo = lax.broadcasted_iota(I32, (1, LANES), 1) < ATT_DH

    def headnorm(x, g):
        x2 = x * x
        lo = jnp.sum(jnp.where(lane_lo, x2, 0.0), axis=-1, keepdims=True)
        hi = jnp.sum(jnp.where(lane_lo, 0.0, x2), axis=-1, keepdims=True)
        ms = jnp.where(lane_lo, lo, hi) * (1.0 / ATT_DH)
        return x * lax.rsqrt(ms + EPS) * g

    q = mm(OFF_QA, OFF_KA)
    k = mm(OFF_KA, OFF_VA)
    v = mm(OFF_VA, OFF_GA)
    for s in range(N_SLABS):
        sl = slice(s * LANES, (s + 1) * LANES)
        vals = (headnorm(q[:, sl], qg_ref[s]) * (ATT_DH ** -0.5), headnorm(k[:, sl], kg_ref[s]), v[:, sl])
        g, sg = divmod(s, SLABS_PER_GROUP)
        dil = ATT_GROUPS[g][1]
        for a, val in enumerate(vals):
            if not dilated:
                rest[a][s] = val
            elif dil == 1:
                rest[3 * g + a][sg] = val
            else:
                scr = rest[-1]
                slot = (s - SLABS_PER_GROUP) * 3 + a
                scr[slot] = val
                for r in range(dil):
                    rest[3 * g + a][sg, :, r * LANES:(r + 1) * LANES] = scr[slot, pl.ds(r, tm // dil, stride=dil), :]


def _inproj(x, g_mix, w_in_bf, qg, kg, dilated):
    n = x.shape[0]
    tm = TM_INPROJ
    if dilated:
        slab_shapes, slab_specs = [], []
        for _, dil in ATT_GROUPS:
            assert tm % (dil * SUBLANES) == 0
            slab_shapes += [jax.ShapeDtypeStruct((SLABS_PER_GROUP, n // dil, _dilated_cols(dil)), F32)] * 3
            slab_specs += [pl.BlockSpec((SLABS_PER_GROUP, tm // dil, _dilated_cols(dil)), lambda i: (0, i, 0))] * 3
        scratch = [pltpu.VMEM(((N_SLABS - SLABS_PER_GROUP) * 3, tm, LANES), F32)]
    else:
        slab_shapes = [jax.ShapeDtypeStruct((N_SLABS, n, LANES), F32)] * 3
        slab_specs = [pl.BlockSpec((N_SLABS, tm, LANES), lambda i: (0, i, 0))] * 3
        scratch = []
    return pl.pallas_call(
        functools.partial(_inproj_kernel, dilated=dilated, tm=tm),
        grid=(n // tm,),
        in_specs=[
            pl.BlockSpec((tm, D_MODEL), lambda i: (i, 0)),
            pl.BlockSpec((1, D_MODEL), lambda i: (0, 0)),
            pl.BlockSpec((D_MODEL, N_IN), lambda i: (0, 0), pipeline_mode=pl.Buffered(1)),
            pl.BlockSpec((N_SLABS, 1, LANES), lambda i: (0, 0, 0)),
            pl.BlockSpec((N_SLABS, 1, LANES), lambda i: (0, 0, 0)),
        ],
        out_specs=[pl.BlockSpec((tm, RET_W), lambda i: (i, 0)), pl.BlockSpec((tm, GATE_W), lambda i: (i, 0))]
        + slab_specs,
        out_shape=[jax.ShapeDtypeStruct((n, RET_W), F32), jax.ShapeDtypeStruct((n, GATE_W), F32)] + slab_shapes,
        scratch_shapes=scratch,
        compiler_params=_cparams(("parallel",)),
        name="inproj_dilated" if dilated else "inproj",
    )(x, g_mix, w_in_bf, qg, kg)


def _ret_tables(chunk):
    lg = jnp.log1p(-jnp.exp2(-5.0 - jnp.arange(RET_HEADS, dtype=F32)))
    pos = jnp.arange(chunk, dtype=F32)
    rel = pos[:, None] - pos[None, :]
    intra = jnp.where(rel[None] >= 0, jnp.exp(lg[:, None, None] * jnp.maximum(rel, 0.0)[None]), 0.0)
    q_dec = jnp.exp(lg[:, None] * (pos[None, :] + 1.0))
    k_dec = jnp.exp(lg[:, None] * (chunk - 1.0 - pos[None, :]))
    c_dec = jnp.exp(lg * chunk)
    q_dec = jnp.broadcast_to(q_dec[:, :, None], (RET_HEADS, chunk, RET_DV))
    k_dec = jnp.broadcast_to(k_dec[:, :, None], (RET_HEADS, chunk, RET_DK))
    c_dec = jnp.broadcast_to(c_dec[:, None, None], (RET_HEADS, 1, RET_DV))
    return intra, q_dec, k_dec, c_dec


def _ret_head(q, kf, v, gr, s, intra, q_dec, k_dec, c_dec, gn):
    qb = q.astype(BF16)
    vb = v.astype(BF16)
    sc = _dot_nt(qb, kf.astype(BF16)) * intra
    o = _dot(sc.astype(BF16), vb) + _dot(qb, s.astype(BF16)) * q_dec
    s_new = s * c_dec + _dot_tn((kf * k_dec).astype(BF16), vb)
    y = _rms(o, gn) * (gr * _sigmoid(gr))
    return y, s_new


def _ret_prompt_kernel(q_ref, k_ref, v_ref, g_ref, intra_ref, qdec_ref, kdec_ref, cdec_ref, gn_ref,
                       o_ref, sfin_ref, s_scr):
    c_idx = pl.program_id(1)

    @pl.when(c_idx == 0)
    def _():
        s_scr[...] = jnp.zeros_like(s_scr)

    for c in range(RET_CHUNKS_PER_STEP):
        rows = slice(c * RET_CHUNK, (c + 1) * RET_CHUNK)
        for h in range(RET_HEADS):
            ks = slice(h * RET_DK, (h + 1) * RET_DK)
            vs = slice(h * RET_DV, (h + 1) * RET_DV)
            y, s_new = _ret_head(q_ref[rows, ks], k_ref[rows, ks], v_ref[rows, vs], g_ref[rows, vs], s_scr[h],
                                 intra_ref[h], qdec_ref[h], kdec_ref[h], cdec_ref[h], gn_ref[h])
            s_scr[h] = s_new
            o_ref[rows, vs] = y.astype(BF16)

    @pl.when(c_idx == pl.num_programs(1) - 1)
    def _():
        sfin_ref[0] = s_scr[...]


def _ret_prompt(zret, ret_gn, batch, seq):
    tc = RET_CHUNK * RET_CHUNKS_PER_STEP
    nc = seq // tc
    intra, q_dec, k_dec, c_dec = _ret_tables(RET_CHUNK)
    row = lambda b, c: b * nc + c
    const3 = lambda b, c: (0, 0, 0)
    return pl.pallas_call(
        _ret_prompt_kernel,
        grid=(batch, nc),
        in_specs=[
            pl.BlockSpec((tc, RET_QK_W), lambda b, c: (row(b, c), OFF_QR // RET_QK_W)),
            pl.BlockSpec((tc, RET_QK_W), lambda b, c: (row(b, c), OFF_KR // RET_QK_W)),
            pl.BlockSpec((tc, RET_V_W), lambda b, c: (row(b, c), OFF_VR // RET_V_W)),
            pl.BlockSpec((tc, RET_V_W), lambda b, c: (row(b, c), OFF_GR // RET_V_W)),
            pl.BlockSpec(intra.shape, const3),
            pl.BlockSpec(q_dec.shape, const3),
            pl.BlockSpec(k_dec.shape, const3),
            pl.BlockSpec(c_dec.shape, const3),
            pl.BlockSpec((RET_HEADS, 1, RET_DV), const3),
        ],
        out_specs=[
            pl.BlockSpec((tc, RET_V_W), lambda b, c: (row(b, c), 0)),
            pl.BlockSpec((1, RET_HEADS, RET_DK, RET_DV), lambda b, c: (b, 0, 0, 0)),
        ],
        out_shape=[
            jax.ShapeDtypeStruct((batch * seq, RET_V_W), BF16),
            jax.ShapeDtypeStruct((batch, RET_HEADS, RET_DK, RET_DV), F32),
        ],
        scratch_shapes=[pltpu.VMEM((RET_HEADS, RET_DK, RET_DV), F32)],
        compiler_params=_cparams(("parallel", "arbitrary")),
        name="ret_prompt",
    )(zret, zret, zret, zret, intra, q_dec, k_dec, c_dec, ret_gn)


def _ret_sample_kernel(q_ref, k_ref, v_ref, g_ref, s_ref, intra_ref, qdec_ref, kdec_ref, cdec_ref, gn_ref,
                       o_ref, snew_ref, *, t):
    for b in range(RET_SAMPLE_BB):
        rows = slice(b * t, (b + 1) * t)
        for h in range(RET_HEADS):
            ks = slice(h * RET_DK, (h + 1) * RET_DK)
            vs = slice(h * RET_DV, (h + 1) * RET_DV)
            y, s_new = _ret_head(q_ref[rows, ks], k_ref[rows, ks], v_ref[rows, vs], g_ref[rows, vs], s_ref[b, h],
                                 intra_ref[h], qdec_ref[h], kdec_ref[h], cdec_ref[h], gn_ref[h])
            snew_ref[b, h] = s_new
            o_ref[rows, vs] = y.astype(BF16)


def _ret_sample(zret, state, ret_gn, batch, t):
    bb = RET_SAMPLE_BB
    tr = bb * t
    intra, q_dec, k_dec, c_dec = _ret_tables(t)
    const3 = lambda i: (0, 0, 0)
    st_spec = pl.BlockSpec((bb, RET_HEADS, RET_DK, RET_DV), lambda i: (i, 0, 0, 0))
    return pl.pallas_call(
        functools.partial(_ret_sample_kernel, t=t),
        grid=(batch // bb,),
        in_specs=[
            pl.BlockSpec((tr, RET_QK_W), lambda i: (i, OFF_QR // RET_QK_W)),
            pl.BlockSpec((tr, RET_QK_W), lambda i: (i, OFF_KR // RET_QK_W)),
            pl.BlockSpec((tr, RET_V_W), lambda i: (i, OFF_VR // RET_V_W)),
            pl.BlockSpec((tr, RET_V_W), lambda i: (i, OFF_GR // RET_V_W)),
            st_spec,
            pl.BlockSpec(intra.shape, const3),
            pl.BlockSpec(q_dec.shape, const3),
            pl.BlockSpec(k_dec.shape, const3),
            pl.BlockSpec(c_dec.shape, const3),
            pl.BlockSpec((RET_HEADS, 1, RET_DV), const3),
        ],
        out_specs=[pl.BlockSpec((tr, RET_V_W), lambda i: (i, 0)), st_spec],
        out_shape=[
            jax.ShapeDtypeStruct((batch * t, RET_V_W), BF16),
            jax.ShapeDtypeStruct((batch, RET_HEADS, RET_DK, RET_DV), F32),
        ],
        compiler_params=_cparams(("parallel",)),
        name="ret_sample",
    )(zret, zret, zret, zret, state, intra, q_dec, k_dec, c_dec, ret_gn)


def _alibi_slopes():
    return jnp.exp2(-8.0 * (jnp.arange(ATT_HEADS, dtype=F32) + 1.0) / ATT_HEADS)


def _softmax_parts(s):
    m = jnp.max(s, axis=-1, keepdims=True)
    e = jnp.exp(s - m)
    den = jnp.sum(e, axis=-1, keepdims=True)
    return e / den, m + jnp.log(den)


def _attn_prompt_kernel(q_ref, kp_ref, kc_ref, vp_ref, vc_ref, tbl_ref, o_ref, lse_ref, *, qb, nres):
    i = pl.program_id(3)
    lane_lo = lax.broadcasted_iota(I32, (1, LANES), 1) < ATT_DH
    kcol = lax.broadcasted_iota(I32, (1, 2 * BAND), 1)
    first = jnp.where((i == 0) & (kcol < BAND), NEG_INF, 0.0)
    for rr in range(nres):
        lanes = slice(rr * LANES, (rr + 1) * LANES)
        k_all = jnp.concatenate([kp_ref[0, :, lanes], kc_ref[0, :, lanes]], axis=0).astype(BF16)
        v_all = jnp.concatenate([vp_ref[0, :, lanes], vc_ref[0, :, lanes]], axis=0).astype(BF16)
        for j in range(qb):
            rows = slice(j * BAND, (j + 1) * BAND)
            q = q_ref[0, rows, lanes]
            k = k_all[j * BAND:(j + 2) * BAND]
            v = v_all[j * BAND:(j + 2) * BAND]
            outs, lses = [], []
            for hh in range(2):
                keep = lane_lo if hh == 0 else jnp.logical_not(lane_lo)
                qm = jnp.where(keep, q, 0.0).astype(BF16)
                s = _dot_nt(qm, k) + tbl_ref[0, hh]
                if j == 0:
                    s = s + first
                p, lse = _softmax_parts(s)
                outs.append(_dot(p.astype(BF16), v))
                lses.append(lse)
            o_ref[0, rows, lanes] = jnp.where(lane_lo, outs[0], outs[1])
            lse_ref[0, rows, lanes] = jnp.where(lane_lo, lses[0], lses[1])


def _attn_prompt(qa, ka, va, group, batch, seq):
    win, dil = ATT_GROUPS[group]
    n_back = win // dil
    nb = seq // dil // BAND
    qb = min(ATTN_Q_BLOCKS, nb)
    nbq = nb // qb
    nres = min(dil, ATTN_Q_BLOCKS // qb)
    slopes = _alibi_slopes()[group * ATT_HPG:(group + 1) * ATT_HPG]
    qi = jnp.arange(BAND)[:, None]
    kc = jnp.arange(2 * BAND)[None, :]
    j = qi - kc + BAND
    valid = (j >= 0) & (j <= n_back)
    tbl = jnp.where(valid[None], -slopes[:, None, None] * (dil * j).astype(F32)[None], NEG_INF)
    tbl = tbl.reshape(SLABS_PER_GROUP, 2, BAND, 2 * BAND)

    cur = lambda s, b, r, i: (s, b * nbq + i, r)
    prev = lambda s, b, r, i: (s, b * nb + jnp.maximum(i * qb - 1, 0), r)
    cur_blk = pl.BlockSpec((1, qb * BAND, nres * LANES), cur)
    prev_blk = pl.BlockSpec((1, BAND, nres * LANES), prev)
    out = jax.ShapeDtypeStruct(qa.shape, F32)
    return pl.pallas_call(
        functools.partial(_attn_prompt_kernel, qb=qb, nres=nres),
        grid=(SLABS_PER_GROUP, batch, dil // nres, nbq),
        in_specs=[
            cur_blk, prev_blk, cur_blk, prev_blk, cur_blk,
            pl.BlockSpec((1, 2, BAND, 2 * BAND), lambda s, b, r, i: (s, 0, 0, 0)),
        ],
        out_specs=[cur_blk, cur_blk],
        out_shape=[out, out],
        compiler_params=_cparams(("parallel", "parallel", "parallel", "parallel")),
        name=f"attn_prompt_g{group}",
    )(qa, ka, ka, va, va, tbl)


def _shift_cache(cache_ref, newc_ref, b, tail, t):
    ncol = cache_ref.shape[-1] // LANES
    keep_lanes = lax.broadcasted_iota(I32, (1, LANES), 1) < LANES - t
    prev_rot = None
    for c in range(ncol):
        rot = pltpu.roll(cache_ref[b, :, c * LANES:(c + 1) * LANES], LANES - t, axis=1)
        if c > 0:
            newc_ref[b, :, (c - 1) * LANES:c * LANES] = jnp.where(keep_lanes, prev_rot, rot)
        prev_rot = rot
    newc_ref[b, :, (ncol - 1) * LANES:ncol * LANES] = jnp.where(keep_lanes, prev_rot, tail)


def _attn_sample_kernel(cache_ref, q_ref, kn_ref, vn_ref, tblc_ref, tbln_ref, newc_ref, o_ref, lse_ref, *, wc, t, bb):
    kw = ATT_OUT_W
    rows = ATT_HPG * t
    rhead = lax.broadcasted_iota(I32, (rows, kw), 0) // t
    lhead = lax.broadcasted_iota(I32, (rows, kw), 1) // ATT_DH
    hm = rhead == lhead
    for b in range(bb):
        tok = slice(b * t, (b + 1) * t)
        kn = jnp.concatenate([kn_ref[0, tok], kn_ref[1, tok]], axis=1)
        vn = jnp.concatenate([vn_ref[0, tok], vn_ref[1, tok]], axis=1)
        pad = jnp.zeros((BAND - t, kw), F32)
        knp = jnp.concatenate([kn, pad], axis=0)
        vnp = jnp.concatenate([vn, pad], axis=0)

        new_t = jnp.concatenate([knp.T, vnp.T], axis=0)
        tail = pltpu.roll(new_t, LANES - t, axis=1)
        _shift_cache(cache_ref, newc_ref, b, tail, t)

        q = jnp.concatenate([q_ref[0, tok], q_ref[1, tok]], axis=1)
        qb = jnp.where(hm, jnp.concatenate([q] * ATT_HPG, axis=0), 0.0).astype(BF16)
        kt = cache_ref[b, 0:kw, :].astype(BF16)
        vt = cache_ref[b, kw:2 * kw, :].astype(BF16)
        s_c = _dot(qb, kt) + tblc_ref[...]
        s_n = _dot_nt(qb, knp.astype(BF16)) + tbln_ref[...]
        m = jnp.maximum(jnp.max(s_c, axis=-1, keepdims=True), jnp.max(s_n, axis=-1, keepdims=True))
        e_c = jnp.exp(s_c - m)
        e_n = jnp.exp(s_n - m)
        den = jnp.sum(e_c, axis=-1, keepdims=True) + jnp.sum(e_n, axis=-1, keepdims=True)
        o = _dot_nt((e_c / den).astype(BF16), vt) + _dot((e_n / den).astype(BF16), vnp.astype(BF16))
        lse = m + jnp.log(den)
        o = jnp.where(hm, o, 0.0)
        lse = jnp.where(hm, lse, 0.0)
        o_sel = o[0:t]
        lse_sel = lse[0:t]
        for h in range(1, ATT_HPG):
            o_sel = o_sel + o[h * t:(h + 1) * t]
            lse_sel = lse_sel + lse[h * t:(h + 1) * t]
        for s in range(SLABS_PER_GROUP):
            o_ref[s, tok, :] = o_sel[:, s * LANES:(s + 1) * LANES]
            lse_ref[s, tok, :] = lse_sel[:, s * LANES:(s + 1) * LANES]


def _cache_view(cache):
    batch, wc = cache.shape[:2]
    return jnp.transpose(cache, (0, 2, 3, 4, 1)).reshape(batch, 2 * ATT_OUT_W, wc)


def _cache_unview(cache_t):
    batch, _, wc = cache_t.shape
    return jnp.transpose(cache_t.reshape(batch, 2, ATT_HPG, ATT_DH, wc), (0, 4, 1, 2, 3))


def _attn_sample(qa, ka, va, cache, group, batch, t):
    win, dil = ATT_GROUPS[group]
    n_back = win // dil
    wc = cache.shape[1]
    assert wc == win and wc % LANES == 0 and t % SUBLANES == 0 and t <= BAND
    kw = ATT_OUT_W
    bb = max(1, min(ATTN_SAMPLE_BLOCK_BYTES // (2 * kw * wc * 4), ATTN_SAMPLE_MAX_BB, batch))
    assert batch % bb == 0
    slopes = _alibi_slopes()[group * ATT_HPG:(group + 1) * ATT_HPG]
    qi = jnp.arange(t)[:, None]
    idx = jnp.concatenate([jnp.arange(wc), wc + jnp.arange(BAND)])[None, :]
    dist = wc + qi - idx
    valid = (dist >= 0) & (dist % dil == 0) & (dist <= dil * n_back) & (idx < wc + t)
    tbl = jnp.where(valid[None], -slopes[:, None, None] * dist.astype(F32)[None], NEG_INF)
    tbl = tbl.reshape(ATT_HPG * t, wc + BAND)
    tblc, tbln = tbl[:, :wc], tbl[:, wc:]

    s0 = group
    new_spec = pl.BlockSpec((SLABS_PER_GROUP, bb * t, LANES), lambda b: (s0, b, 0))
    out_spec = pl.BlockSpec((SLABS_PER_GROUP, bb * t, LANES), lambda b: (0, b, 0))
    cache_spec = pl.BlockSpec((bb, 2 * kw, wc), lambda b: (b, 0, 0))
    out = jax.ShapeDtypeStruct((SLABS_PER_GROUP, batch * t, LANES), F32)
    newc, o, lse = pl.pallas_call(
        functools.partial(_attn_sample_kernel, wc=wc, t=t, bb=bb),
        grid=(batch // bb,),
        in_specs=[
            cache_spec, new_spec, new_spec, new_spec,
            pl.BlockSpec(tblc.shape, lambda b: (0, 0)),
            pl.BlockSpec(tbln.shape, lambda b: (0, 0)),
        ],
        out_specs=[cache_spec, out_spec, out_spec],
        out_shape=[jax.ShapeDtypeStruct((batch, 2 * kw, wc), F32), out, out],
        compiler_params=_cparams(("parallel",)),
        name=f"attn_sample_g{group}",
    )(_cache_view(cache), qa, ka, va, tblc, tbln)
    return o, lse, _cache_unview(newc)


def _merge_kernel(x_ref, oret_ref, gate_ref, o0_ref, o1_ref, o2_ref, l0_ref, l1_ref, l2_ref,
                  wa_ref, wb_ref, wo_ref, x1_ref, *scratch, dilated, tm):
    def rows(ref, g, s, kind):
        dil = ATT_GROUPS[g][1]
        if not dilated or dil == 1:
            return ref[s]
        scr = scratch[0]
        slot = ((g - 1) * SLABS_PER_GROUP + s) * 2 + kind
        for r in range(dil):
            scr[slot, pl.ds(r, tm // dil, stride=dil), :] = ref[s, :, r * LANES:(r + 1) * LANES]
        return scr[slot]

    slabs = []
    for s in range(SLABS_PER_GROUP):
        ls = [rows(ref, g, s, 0) for g, ref in enumerate((l0_ref, l1_ref, l2_ref))]
        os_ = [rows(ref, g, s, 1) for g, ref in enumerate((o0_ref, o1_ref, o2_ref))]
        m = jnp.maximum(jnp.maximum(ls[0], ls[1]), ls[2])
        es = [jnp.exp(l - m) for l in ls]
        den = es[0] + es[1] + es[2]
        slabs.append((es[0] / den) * os_[0] + (es[1] / den) * os_[1] + (es[2] / den) * os_[2])
    o_att = jnp.concatenate(slabs, axis=1).astype(BF16)
    ga = gate_ref[:, 0:D_MODEL]
    gb = gate_ref[:, D_MODEL:GATE_W]
    merged = _sigmoid(ga) * _dot(oret_ref[...], wa_ref[...]) + _sigmoid(gb) * _dot(o_att, wb_ref[...])
    x1_ref[...] = x_ref[...] + _dot(merged.astype(BF16), wo_ref[...])


def _merge(x, oret, zgate, os_, ls, wa, wb, wo, dilated):
    n = x.shape[0]
    tm = min(TM_MERGE, n)
    row = lambda w: pl.BlockSpec((tm, w), lambda i: (i, 0))
    full = lambda a: pl.BlockSpec(a.shape, lambda i: (0, 0))

    def slab(g):
        dil = ATT_GROUPS[g][1] if dilated else 1
        return pl.BlockSpec((SLABS_PER_GROUP, tm // dil, dil * LANES), lambda i: (0, i, 0))

    slabs = [slab(g) for g in range(len(ATT_GROUPS))]
    n_scr = (len(ATT_GROUPS) - 1) * SLABS_PER_GROUP * 2
    return pl.pallas_call(
        functools.partial(_merge_kernel, dilated=dilated, tm=tm),
        grid=(n // tm,),
        in_specs=[row(D_MODEL), row(RET_V_W), row(GATE_W)] + slabs + slabs + [full(wa), full(wb), full(wo)],
        out_specs=row(D_MODEL),
        out_shape=jax.ShapeDtypeStruct((n, D_MODEL), F32),
        scratch_shapes=[pltpu.VMEM((n_scr, tm, LANES), F32)] if dilated else [],
        compiler_params=_cparams(("parallel",)),
        name="merge_dilated" if dilated else "merge",
    )(x, oret, zgate, *os_, *ls, wa, wb, wo)


def _router_kernel(xp_ref, xs_ref, g_ref, wr_ref, br_ref, tri_ref, h_ref, idx_ref, rank_ref, w_ref, cnt_ref,
                   run_scr, *, n_prompt_tiles, tm):
    i = pl.program_id(0)

    @pl.when(i == 0)
    def _():
        run_scr[...] = jnp.zeros_like(run_scr)

    x = jnp.where(i < n_prompt_tiles, xp_ref[...], xs_ref[...])
    h = _rms(x, g_ref[...])
    for j in range(ROW_TILES):
        h_ref[pl.ds(j, tm, stride=ROW_TILES), :] = h[:, j * LANES:(j + 1) * LANES]

    logits = jnp.dot(h, wr_ref[...], precision=lax.Precision.HIGHEST, preferred_element_type=F32) + br_ref[...]
    lanes = lax.broadcasted_iota(I32, (1, LANES), 1)
    lanes_f = lanes.astype(F32)
    work = logits
    vals, idxs, hots = [], [], []
    for _ in range(TOP_K):
        m = jnp.max(work, axis=-1, keepdims=True)
        idx = jnp.min(jnp.where(work == m, lanes_f, float(LANES)), axis=-1, keepdims=True)
        hot = lanes_f == idx
        vals.append(m)
        idxs.append(idx)
        hots.append(hot)
        work = jnp.where(hot, -jnp.inf, work)
    exps = [jnp.exp(v - vals[0]) for v in vals]
    den = exps[0] + exps[1] + exps[2] + exps[3]

    cat = jnp.concatenate([hot.astype(BF16) for hot in hots], axis=1)
    cum = _dot(tri_ref[...], cat)
    prev = run_scr[...]
    idx_out = jnp.zeros((tm, LANES), I32)
    rank_out = jnp.zeros((tm, LANES), I32)
    w_out = jnp.zeros((tm, LANES), F32)
    for k in range(TOP_K):
        hot_f = hots[k].astype(F32)
        rank = jnp.sum(hot_f * (cum[:, k * LANES:(k + 1) * LANES] + prev), axis=-1, keepdims=True)
        prev = prev + jnp.sum(hot_f, axis=0, keepdims=True)
        idx_out = jnp.where(lanes == k, idxs[k].astype(I32), idx_out)
        rank_out = jnp.where(lanes == k, rank.astype(I32), rank_out)
        w_out = jnp.where(lanes == k, exps[k] / den, w_out)
    run_scr[...] = prev
    idx_ref[...] = idx_out
    rank_ref[...] = rank_out
    w_ref[...] = w_out
    cnt_ref[...] = prev.astype(I32)


def _router(x1p, x1s, g_moe, wr_pad, br_pad):
    tm = TM_ROUTE
    npt, nst = x1p.shape[0] // tm, x1s.shape[0] // tm
    n = x1p.shape[0] + x1s.shape[0]
    tri = (jnp.arange(tm)[:, None] > jnp.arange(tm)[None, :]).astype(BF16)
    row128 = pl.BlockSpec((tm, LANES), lambda i: (i, 0))
    return pl.pallas_call(
        functools.partial(_router_kernel, n_prompt_tiles=npt, tm=tm),
        grid=(npt + nst,),
        in_specs=[
            pl.BlockSpec((tm, D_MODEL), lambda i: (jnp.minimum(i, npt - 1), 0)),
            pl.BlockSpec((tm, D_MODEL), lambda i: (jnp.clip(i - npt, 0, nst - 1), 0)),
            pl.BlockSpec((1, D_MODEL), lambda i: (0, 0)),
            pl.BlockSpec((D_MODEL, LANES), lambda i: (0, 0)),
            pl.BlockSpec((1, LANES), lambda i: (0, 0)),
            pl.BlockSpec((tm, tm), lambda i: (0, 0)),
        ],
        out_specs=[
            pl.BlockSpec((tm * ROW_TILES, LANES), lambda i: (i, 0)),
            row128, row128, row128,
            pl.BlockSpec((1, LANES), lambda i: (0, 0)),
        ],
        out_shape=[
            jax.ShapeDtypeStruct((n * ROW_TILES, LANES), F32),
            jax.ShapeDtypeStruct((n, LANES), I32),
            jax.ShapeDtypeStruct((n, LANES), I32),
            jax.ShapeDtypeStruct((n, LANES), F32),
            jax.ShapeDtypeStruct((1, LANES), I32),
        ],
        scratch_shapes=[pltpu.VMEM((1, LANES), F32)],
        compiler_params=_cparams(("arbitrary",)),
        name="router",
    )(x1p, x1s, g_moe, wr_pad, br_pad, tri)


def _row_copy(src_ref, src_row, dst_ref, dst_row, sem):
    return pltpu.make_async_copy(
        src_ref.at[pl.ds(pl.multiple_of(src_row * ROW_TILES, ROW_TILES), ROW_TILES), :],
        dst_ref.at[pl.ds(pl.multiple_of(dst_row * ROW_TILES, ROW_TILES), ROW_TILES), :],
        sem)


def _dispatch_kernel(zstart_ref, zlen_ref, nused_ref, slots_ref, h_ref, hs_ref, zero_scr, sem, *, tm, n_tiles):
    i = pl.program_id(0)

    def zero_copy(row, n_rows):
        start = pl.multiple_of(row * ROW_TILES, ROW_TILES)
        return pltpu.make_async_copy(zero_scr.at[pl.ds(0, n_rows * ROW_TILES), :],
                                     hs_ref.at[pl.ds(start, n_rows * ROW_TILES), :], sem)

    def pad_copies(e, op):
        row, left = zstart_ref[e], zlen_ref[e]
        for b in reversed(range(TM_MOE.bit_length() - 1)):
            bit = (left >> b) & 1

            @pl.when(bit == 1)
            def _():
                op(zero_copy(row, 1 << b))

            row = row + (bit << b)

    def all_copies(op):
        lax.fori_loop(0, N_EXPERTS, lambda e, c: (pad_copies(e, op), c)[1], 0)
        lax.fori_loop(nused_ref[0], n_tiles, lambda tt, c: (op(zero_copy(tt * TM_MOE, TM_MOE)), c)[1], 0)

    @pl.when(i == 0)
    def _():
        zero_scr[...] = jnp.zeros_like(zero_scr)
        all_copies(lambda c: c.start())
        all_copies(lambda c: c.wait())

    def body(tok, carry):
        for k in range(TOP_K):
            _row_copy(h_ref, tok, hs_ref, slots_ref[0, 0, tok * TOP_K + k], sem).start(priority=k % 2)
        return carry

    lax.fori_loop(0, tm, body, 0, unroll=8)
    for _ in range(TOP_K):
        pltpu.make_async_copy(h_ref, hs_ref.at[pl.ds(0, tm * ROW_TILES), :], sem).wait()


def _dispatch(hrow, slots, zstart, zlen, n_used, n_tiles):
    tm = TM_ROUTE
    n = hrow.shape[0] // ROW_TILES
    nt = n // tm
    slots3 = slots.reshape(nt, 1, tm * TOP_K)
    grid_spec = pltpu.PrefetchScalarGridSpec(
        num_scalar_prefetch=3,
        grid=(nt,),
        in_specs=[
            pl.BlockSpec((1, 1, tm * TOP_K), lambda i, z, l, u: (i, 0, 0), memory_space=pltpu.SMEM),
            pl.BlockSpec((tm * ROW_TILES, LANES), lambda i, z, l, u: (i, 0)),
        ],
        out_specs=pl.BlockSpec(memory_space=pl.ANY),
        scratch_shapes=[pltpu.VMEM((TM_MOE * ROW_TILES, LANES), F32), pltpu.SemaphoreType.DMA(())],
    )
    return pl.pallas_call(
        functools.partial(_dispatch_kernel, tm=tm, n_tiles=n_tiles),
        grid_spec=grid_spec,
        out_shape=jax.ShapeDtypeStruct((n_tiles * TM_MOE * ROW_TILES, LANES), F32),
        compiler_params=_cparams(("arbitrary",)),
        name="dispatch",
    )(zstart, zlen, n_used, slots3, hrow)


def _experts_kernel(texp_ref, nused_ref, hs_ref, wu_ref, bu_ref, wd_ref, bd_ref, ys_ref, wu_scr, wd_scr):
    j = pl.program_id(0)
    tm = TM_MOE
    n_used = nused_ref[0]
    jj = jnp.minimum(j, n_used - 1)
    new_expert = (j == 0) | (texp_ref[jj] != texp_ref[jnp.maximum(jj - 1, 0)])

    @pl.when((j < n_used) & new_expert)
    def _():
        wu_scr[...] = wu_ref[0].astype(BF16)
        wd_scr[...] = wd_ref[0].astype(BF16)

    @pl.when(j < n_used)
    def _():
        x = jnp.concatenate(
            [hs_ref[pl.ds(c, tm, stride=ROW_TILES), :] for c in range(ROW_TILES)], axis=1).astype(BF16)
        u = _dot(x, wu_scr[...]) + bu_ref[0]
        glu = jnp.minimum(u[:, :D_EXPERT], SWIGLU_LIMIT)
        lin = jnp.clip(u[:, D_EXPERT:], -SWIGLU_LIMIT, SWIGLU_LIMIT)
        a = glu * _sigmoid(SWIGLU_ALPHA * glu) * (lin + 1.0)
        y = _dot(a.astype(BF16), wd_scr[...]) + bd_ref[0]
        for c in range(ROW_TILES):
            ys_ref[pl.ds(c, tm, stride=ROW_TILES), :] = y[:, c * LANES:(c + 1) * LANES]

    @pl.when(j >= n_used)
    def _():
        ys_ref[...] = jnp.zeros_like(ys_ref)


def _experts(hs, tile_expert, n_used, w_up, b_up, w_down, b_down, n_tiles):
    tm = TM_MOE
    tile = lambda j, te, nu: (jnp.minimum(j, nu[0] - 1), 0)
    exp3 = lambda j, te, nu: (te[jnp.minimum(j, nu[0] - 1)], 0, 0)
    grid_spec = pltpu.PrefetchScalarGridSpec(
        num_scalar_prefetch=2,
        grid=(n_tiles,),
        in_specs=[
            pl.BlockSpec((tm * ROW_TILES, LANES), tile),
            pl.BlockSpec((1, D_MODEL, 2 * D_EXPERT), exp3),
            pl.BlockSpec((1, 1, 2 * D_EXPERT), exp3),
            pl.BlockSpec((1, D_EXPERT, D_MODEL), exp3),
            pl.BlockSpec((1, 1, D_MODEL), exp3),
        ],
        out_specs=pl.BlockSpec((tm * ROW_TILES, LANES), lambda j, te, nu: (j, 0)),
        scratch_shapes=[pltpu.VMEM((D_MODEL, 2 * D_EXPERT), BF16), pltpu.VMEM((D_EXPERT, D_MODEL), BF16)],
    )
    return pl.pallas_call(
        _experts_kernel,
        grid_spec=grid_spec,
        out_shape=jax.ShapeDtypeStruct((n_tiles * tm * ROW_TILES, LANES), F32),
        compiler_params=_cparams(("arbitrary",)),
        name="experts",
    )(tile_expert, n_used, hs, w_up, b_up.reshape(N_EXPERTS, 1, -1), w_down, b_down.reshape(N_EXPERTS, 1, -1))


def _combine_kernel(slots_ref, nslots_ref, x1_ref, w_ref, p_ref, g_ref, wpg_ref, wpe_ref, ys_ref, y_ref,
                    buf, x2_scr, sems, *, tm):
    i = pl.program_id(0)
    cur = lax.rem(i, 2)
    nxt = 1 - cur

    def gather(sref, half, tok, k):
        return _row_copy(ys_ref, sref[0, 0, tok * TOP_K + k], buf.at[half, k], tok, sems.at[half])

    def wait_tile(half):
        for k in range(TOP_K):
            pltpu.make_async_copy(ys_ref.at[pl.ds(0, tm * ROW_TILES), :], buf.at[half, k], sems.at[half]).wait()

    @pl.when(i == 0)
    def _():
        def body(tok, carry):
            for k in range(TOP_K):
                gather(slots_ref, 0, tok, k).start(priority=k % 2)
            return carry

        lax.fori_loop(0, tm, body, 0)

    wait_tile(cur)
    rg = COMB_ROW_GROUP
    per = tm // (tm // rg * ROW_TILES)
    for r in range(tm // rg):
        w = w_ref[r * rg:(r + 1) * rg, :]
        wk = [jnp.broadcast_to(w[:, k:k + 1], (rg, LANES)) for k in range(TOP_K)]
        for c in range(ROW_TILES):
            acc = x1_ref[r * rg:(r + 1) * rg, c * LANES:(c + 1) * LANES]
            for k in range(TOP_K):
                acc = acc + wk[k] * buf[cur, k, pl.ds(r * rg * ROW_TILES + c, rg, stride=ROW_TILES), :]
            g = r * ROW_TILES + c
            for tok in range(g * per, (g + 1) * per):
                for k in range(TOP_K):
                    gather(nslots_ref, nxt, tok, k).start(priority=k % 2)
            x2_scr[r * rg:(r + 1) * rg, c * LANES:(c + 1) * LANES] = acc
    x2 = x2_scr[...]
    gate = _sigmoid(_dot(_rms(x2, g_ref[...]).astype(BF16), wpg_ref[...]))
    y_ref[...] = x2 + gate * _dot(p_ref[...].astype(BF16), wpe_ref[...])

    @pl.when(i == pl.num_programs(0) - 1)
    def _():
        wait_tile(nxt)


def _combine(x1, slots, topw, p, g_ple, wpg, wpe, ys):
    n = x1.shape[0]
    tm = TM_COMB
    nt = n // tm
    slots3 = slots.reshape(nt, 1, tm * TOP_K)
    full = lambda a: pl.BlockSpec(a.shape, lambda i: (0, 0))
    return pl.pallas_call(
        functools.partial(_combine_kernel, tm=tm),
        grid=(nt,),
        in_specs=[
            pl.BlockSpec((1, 1, tm * TOP_K), lambda i: (i, 0, 0), memory_space=pltpu.SMEM),
            pl.BlockSpec((1, 1, tm * TOP_K), lambda i: (jnp.minimum(i + 1, nt - 1), 0, 0), memory_space=pltpu.SMEM),
            pl.BlockSpec((tm, D_MODEL), lambda i: (i, 0)),
            pl.BlockSpec((tm, LANES), lambda i: (i, 0)),
            pl.BlockSpec((tm, PLE_DIM), lambda i: (i, 0)),
            full(g_ple), full(wpg), full(wpe),
            pl.BlockSpec(memory_space=pl.ANY),
        ],
        out_specs=pl.BlockSpec((tm, D_MODEL), lambda i: (i, 0)),
        out_shape=jax.ShapeDtypeStruct((n, D_MODEL), F32),
        scratch_shapes=[pltpu.VMEM((2, TOP_K, tm * ROW_TILES, LANES), F32), pltpu.VMEM((tm, D_MODEL), F32),
                        pltpu.SemaphoreType.DMA((2,))],
        compiler_params=_cparams(("arbitrary",)),
        name="combine",
    )(slots3, slots3, x1, topw, p, g_ple, wpg, wpe, ys)


def _cache_from_slabs(ka, va, group, batch, seq, keep):
    dil = ATT_GROUPS[group][1]
    assert keep % dil == 0

    def tail(a):
        a = a.reshape(SLABS_PER_GROUP, batch, seq // dil, dil * LANES)[:, :, (seq - keep) // dil:]
        a = a.reshape(SLABS_PER_GROUP, batch, keep // dil, dil, LANES)
        return jnp.transpose(a, (1, 2, 3, 0, 4)).reshape(batch, keep, ATT_HPG, ATT_DH)

    return jnp.stack([tail(ka), tail(va)], axis=2)


def _layer(xp, xs, pp, ps, state, caches, g_mix, w_in, ret_gn, qn_g, kn_g, w_a, w_b, w_o, g_moe, w_router, b_router,
           w_up, b_up, w_down, b_down, g_ple, w_ple_gate, w_ple):
    batch, seq, _ = xp.shape
    dbatch, dseq, _ = xs.shape
    n_p, n_s = batch * seq, dbatch * dseq
    n = n_p + n_s

    w_in_bf = w_in.astype(BF16)
    g_mix2 = g_mix.reshape(1, D_MODEL)
    qg = qn_g.reshape(N_SLABS, 1, LANES)
    kg = kn_g.reshape(N_SLABS, 1, LANES)
    gn = ret_gn.reshape(RET_HEADS, 1, RET_DV)
    wa, wb, wo = w_a.astype(BF16), w_b.astype(BF16), w_o.astype(BF16)

    xp2 = xp.reshape(n_p, D_MODEL)
    xs2 = xs.reshape(n_s, D_MODEL)
    zret_p, zgate_p, *qkv_p = _inproj(xp2, g_mix2, w_in_bf, qg, kg, dilated=True)
    zret_s, zgate_s, qa_s, ka_s, va_s = _inproj(xs2, g_mix2, w_in_bf, qg, kg, dilated=False)

    oret_p, state_p = _ret_prompt(zret_p, gn, batch, seq)
    oret_s, state_s = _ret_sample(zret_s, state, gn, dbatch, dseq)

    o_p, l_p, o_s, l_s, win_p, win_s = [], [], [], [], [], []
    for g, (win, _) in enumerate(ATT_GROUPS):
        qa_g, ka_g, va_g = qkv_p[3 * g:3 * g + 3]
        o, l = _attn_prompt(qa_g, ka_g, va_g, g, batch, seq)
        o_p.append(o)
        l_p.append(l)
        win_p.append(_cache_from_slabs(ka_g, va_g, g, batch, seq, min(win, seq)))
        o, l, newc = _attn_sample(qa_s, ka_s, va_s, caches[g], g, dbatch, dseq)
        o_s.append(o)
        l_s.append(l)
        win_s.append(newc)

    x1p = _merge(xp2, oret_p, zgate_p, o_p, l_p, wa, wb, wo, dilated=True)
    x1s = _merge(xs2, oret_s, zgate_s, o_s, l_s, wa, wb, wo, dilated=False)

    wr_pad = jnp.zeros((D_MODEL, LANES), F32).at[:, :N_EXPERTS].set(w_router)
    br_pad = jnp.full((1, LANES), NEG_INF, F32).at[0, :N_EXPERTS].set(b_router)
    hrow, top_i, rank, top_w, counts = _router(x1p, x1s, g_moe.reshape(1, D_MODEL), wr_pad, br_pad)

    cnt = counts[0, :N_EXPERTS]
    padded = (cnt + TM_MOE - 1) // TM_MOE * TM_MOE
    ends = jnp.cumsum(padded)
    goff = ends - padded
    slots = goff[top_i[:, :TOP_K]] + rank[:, :TOP_K]
    n_tiles = (n * TOP_K) // TM_MOE + N_EXPERTS
    tile_ends = ends // TM_MOE
    n_used = tile_ends[-1:].astype(I32)
    tile_expert = jnp.sum(tile_ends[None, :] <= jnp.arange(n_tiles)[:, None], axis=1)
    tile_expert = jnp.minimum(tile_expert, N_EXPERTS - 1).astype(I32)
    zstart = (goff + cnt).astype(I32)
    zlen = (padded - cnt).astype(I32)

    hs = _dispatch(hrow, slots, zstart, zlen, n_used, n_tiles)
    ys = _experts(hs, tile_expert, n_used, w_up, b_up, w_down, b_down, n_tiles)

    g_ple2 = g_ple.reshape(1, D_MODEL)
    wpg, wpe = w_ple_gate.astype(BF16), w_ple.astype(BF16)
    yp = _combine(x1p, slots[:n_p], top_w[:n_p], pp.reshape(n_p, PLE_DIM), g_ple2, wpg, wpe, ys)
    ys_out = _combine(x1s, slots[n_p:], top_w[n_p:], ps.reshape(n_s, PLE_DIM), g_ple2, wpg, wpe, ys)
    return (yp.reshape(batch, seq, D_MODEL), ys_out.reshape(dbatch, dseq, D_MODEL), state_p, state_s, win_p, win_s)


def kernel(x_prompt, x_sample, p_prompt, p_sample, state_ret, cache_win128_kv, cache_win512_kv, cache_win2048_kv, norm_mix_g, w_in, ret_norm_g, q_norm_g, k_norm_g, w_a, w_b, w_o, norm_moe_g, w_router, b_router, w_up, b_up, w_down, b_down, norm_ple_g, w_ple_gate, w_ple):
    caches = (cache_win128_kv, cache_win512_kv, cache_win2048_kv)
    depth = w_in.shape[0]
    xp, xs = x_prompt, x_sample
    ret_p, ret_s = [], []
    win_p = [[] for _ in ATT_GROUPS]
    win_s = [[] for _ in ATT_GROUPS]
    for i in range(depth):
        xp, xs, sp, ss, bp, bs = _layer(
            xp, xs, p_prompt[i], p_sample[i], state_ret[i], [c[i] for c in caches], norm_mix_g[i], w_in[i],
            ret_norm_g[i], q_norm_g[i], k_norm_g[i], w_a[i], w_b[i], w_o[i], norm_moe_g[i], w_router[i], b_router[i],
            w_up[i], b_up[i], w_down[i], b_down[i], norm_ple_g[i], w_ple_gate[i], w_ple[i])
        ret_p.append(sp)
        ret_s.append(ss)
        for g in range(len(ATT_GROUPS)):
            win_p[g].append(bp[g])
            win_s[g].append(bs[g])
    return (xp, xs, jnp.stack(ret_p), jnp.stack(ret_s), jnp.stack(win_p[0]), jnp.stack(win_s[0]),
            jnp.stack(win_p[1]), jnp.stack(win_s[1]), jnp.stack(win_p[2]), jnp.stack(win_s[2]))
```

```python
import functools

import jax
import jax.numpy as jnp
from jax import lax
from jax.experimental import pallas as pl
from jax.experimental.pallas import tpu as pltpu

F32 = jnp.float32
BF16 = jnp.bfloat16
I32 = jnp.int32

D_MODEL = 1024
RET_HEADS = 4
RET_DK = 128
RET_DV = 256
RET_QK_W = RET_HEADS * RET_DK
RET_V_W = RET_HEADS * RET_DV
RET_CHUNK = 128
ATT_GROUPS = ((128, 1), (512, 4), (2048, 16))
ATT_HPG = 4
ATT_HEADS = ATT_HPG * len(ATT_GROUPS)
ATT_DH = 64
ATT_W = ATT_HEADS * ATT_DH
ATT_OUT_W = ATT_HPG * ATT_DH
BAND = 128
N_SLABS = ATT_W // 128
SLABS_PER_GROUP = ATT_OUT_W // 128
N_EXPERTS = 32
TOP_K = 4
D_EXPERT = 1024
SWIGLU_LIMIT = 7.0
SWIGLU_ALPHA = 1.702
PLE_DIM = 256
EPS = 1e-6
NEG_INF = -1e30

OFF_QR = 0
OFF_KR = OFF_QR + RET_QK_W
OFF_VR = OFF_KR + RET_QK_W
OFF_GR = OFF_VR + RET_V_W
OFF_QA = OFF_GR + RET_V_W
OFF_KA = OFF_QA + ATT_W
OFF_VA = OFF_KA + ATT_W
OFF_GA = OFF_VA + ATT_W
OFF_GB = OFF_GA + D_MODEL
N_IN = OFF_GB + D_MODEL
RET_W = OFF_QA
GATE_W = 2 * D_MODEL

LANES = 128
SUBLANES = 8
ROW_TILES = D_MODEL // LANES
VMEM_LIMIT = 56 * 1024 * 1024

TM_INPROJ = 512
TM_MERGE = 512
TM_ROUTE = 512
TM_MOE = 512
TM_COMB = 256
COMB_ROW_GROUP = 64
RET_CHUNKS_PER_STEP = 8
RET_SAMPLE_BB = 8
ATTN_Q_BLOCKS = 8
ATTN_SAMPLE_BLOCK_BYTES = 8 * 1024 * 1024
ATTN_SAMPLE_MAX_BB = 16


def _cparams(sem):
    return pltpu.CompilerParams(dimension_semantics=sem, vmem_limit_bytes=VMEM_LIMIT)


def _dot(a, b):
    return jnp.dot(a, b, preferred_element_type=F32)


def _dot_nt(a, b):
    return lax.dot_general(a, b, (((1,), (1,)), ((), ())), preferred_element_type=F32)


def _dot_tn(a, b):
    return lax.dot_general(a, b, (((0,), (0,)), ((), ())), preferred_element_type=F32)


def _rms(x, g):
    return x * lax.rsqrt(jnp.mean(x * x, axis=-1, keepdims=True) + EPS) * g


def _sigmoid(x):
    return 1.0 / (1.0 + jnp.exp(-x))


def _dilated_cols(dil):
    return dil * LANES


def _inproj_kernel(x_ref, g_ref, w_ref, qg_ref, kg_ref, zret_ref, zgate_ref, *rest, dilated, tm):
    h = _rms(x_ref[...], g_ref[...]).astype(BF16)

    def mm(lo, hi):
        return _dot(h, w_ref[:, lo:hi])

    zret_ref[:, OFF_QR:OFF_KR] = mm(OFF_QR, OFF_KR).astype(BF16)
    zret_ref[:, OFF_KR:OFF_VR] = (mm(OFF_KR, OFF_VR) * (RET_DK ** -0.5)).astype(BF16)
    zret_ref[:, OFF_VR:OFF_QA] = mm(OFF_VR, OFF_QA).astype(BF16)
    zgate_ref[...] = mm(OFF_GA, N_IN).astype(BF16)

    lane_lo = lax.broadcasted_iota(I32, (1, LANES), 1) < ATT_DH

    def headnorm(x, g):
        x2 = x * x
        lo = jnp.sum(jnp.where(lane_lo, x2, 0.0), axis=-1, keepdims=True)
        hi = jnp.sum(jnp.where(lane_lo, 0.0, x2), axis=-1, keepdims=True)
        ms = jnp.where(lane_lo, lo, hi) * (1.0 / ATT_DH)
        return x * lax.rsqrt(ms + EPS) * g

    q = mm(OFF_QA, OFF_KA)
    k = mm(OFF_KA, OFF_VA)
    v = mm(OFF_VA, OFF_GA)
    for s in range(N_SLABS):
        sl = slice(s * LANES, (s + 1) * LANES)
        vals = (headnorm(q[:, sl], qg_ref[s]) * (ATT_DH ** -0.5), headnorm(k[:, sl], kg_ref[s]), v[:, sl])
        g, sg = divmod(s, SLABS_PER_GROUP)
        dil = ATT_GROUPS[g][1]
        for a, val in enumerate(vals):
            if not dilated:
                rest[a][s] = val
            elif dil == 1:
                rest[3 * g + a][sg] = val
            else:
                scr = rest[-1]
                slot = (s - SLABS_PER_GROUP) * 3 + a
                scr[slot] = val
                for r in range(dil):
                    rest[3 * g + a][sg, :, r * LANES:(r + 1) * LANES] = scr[slot, pl.ds(r, tm // dil, stride=dil), :]


def _inproj(x, g_mix, w_in_bf, qg, kg, dilated):
    n = x.shape[0]
    tm = TM_INPROJ
    if dilated:
        slab_shapes, slab_specs = [], []
        for _, dil in ATT_GROUPS:
            assert tm % (dil * SUBLANES) == 0
            slab_shapes += [jax.ShapeDtypeStruct((SLABS_PER_GROUP, n // dil, _dilated_cols(dil)), F32)] * 3
            slab_specs += [pl.BlockSpec((SLABS_PER_GROUP, tm // dil, _dilated_cols(dil)), lambda i: (0, i, 0))] * 3
        scratch = [pltpu.VMEM(((N_SLABS - SLABS_PER_GROUP) * 3, tm, LANES), F32)]
    else:
        slab_shapes = [jax.ShapeDtypeStruct((N_SLABS, n, LANES), F32)] * 3
        slab_specs = [pl.BlockSpec((N_SLABS, tm, LANES), lambda i: (0, i, 0))] * 3
        scratch = []
    return pl.pallas_call(
        functools.partial(_inproj_kernel, dilated=dilated, tm=tm),
        grid=(n // tm,),
        in_specs=[
            pl.BlockSpec((tm, D_MODEL), lambda i: (i, 0)),
            pl.BlockSpec((1, D_MODEL), lambda i: (0, 0)),
            pl.BlockSpec((D_MODEL, N_IN), lambda i: (0, 0), pipeline_mode=pl.Buffered(1)),
            pl.BlockSpec((N_SLABS, 1, LANES), lambda i: (0, 0, 0)),
            pl.BlockSpec((N_SLABS, 1, LANES), lambda i: (0, 0, 0)),
        ],
        out_specs=[pl.BlockSpec((tm, RET_W), lambda i: (i, 0)), pl.BlockSpec((tm, GATE_W), lambda i: (i, 0))]
        + slab_specs,
        out_shape=[jax.ShapeDtypeStruct((n, RET_W), BF16), jax.ShapeDtypeStruct((n, GATE_W), BF16)] + slab_shapes,
        scratch_shapes=scratch,
        compiler_params=_cparams(("parallel",)),
        name="inproj_dilated" if dilated else "inproj",
    )(x, g_mix, w_in_bf, qg, kg)


def _ret_tables(chunk):
    lg = jnp.log1p(-jnp.exp2(-5.0 - jnp.arange(RET_HEADS, dtype=F32)))
    pos = jnp.arange(chunk, dtype=F32)
    rel = pos[:, None] - pos[None, :]
    intra = jnp.where(rel[None] >= 0, jnp.exp(lg[:, None, None] * jnp.maximum(rel, 0.0)[None]), 0.0)
    q_dec = jnp.exp(lg[:, None] * (pos[None, :] + 1.0))
    k_dec = jnp.exp(lg[:, None] * (chunk - 1.0 - pos[None, :]))
    c_dec = jnp.exp(lg * chunk)
    q_dec = jnp.broadcast_to(q_dec[:, :, None], (RET_HEADS, chunk, RET_DV))
    k_dec = jnp.broadcast_to(k_dec[:, :, None], (RET_HEADS, chunk, RET_DK))
    c_dec = jnp.broadcast_to(c_dec[:, None, None], (RET_HEADS, 1, RET_DV))
    return intra, q_dec, k_dec, c_dec


def _ret_head(q, kf, v, gr, s, intra, q_dec, k_dec, c_dec, gn):
    qb = q.astype(BF16)
    vb = v.astype(BF16)
    gr = gr.astype(F32)
    sc = _dot_nt(qb, kf.astype(BF16)) * intra
    o = _dot(sc.astype(BF16), vb) + _dot(qb, s.astype(BF16)) * q_dec
    s_new = s * c_dec + _dot_tn((kf.astype(F32) * k_dec).astype(BF16), vb)
    y = _rms(o, gn) * (gr * _sigmoid(gr))
    return y, s_new


def _ret_prompt_kernel(q_ref, k_ref, v_ref, g_ref, intra_ref, qdec_ref, kdec_ref, cdec_ref, gn_ref,
                       o_ref, sfin_ref, s_scr):
    c_idx = pl.program_id(1)

    @pl.when(c_idx == 0)
    def _():
        s_scr[...] = jnp.zeros_like(s_scr)

    for c in range(RET_CHUNKS_PER_STEP):
        rows = slice(c * RET_CHUNK, (c + 1) * RET_CHUNK)
        for h in range(RET_HEADS):
            ks = slice(h * RET_DK, (h + 1) * RET_DK)
            vs = slice(h * RET_DV, (h + 1) * RET_DV)
            y, s_new = _ret_head(q_ref[rows, ks], k_ref[rows, ks], v_ref[rows, vs], g_ref[rows, vs], s_scr[h],
                                 intra_ref[h], qdec_ref[h], kdec_ref[h], cdec_ref[h], gn_ref[h])
            s_scr[h] = s_new
            o_ref[rows, vs] = y.astype(BF16)

    @pl.when(c_idx == pl.num_programs(1) - 1)
    def _():
        sfin_ref[0] = s_scr[...]


def _ret_prompt(zret, ret_gn, batch, seq):
    tc = RET_CHUNK * RET_CHUNKS_PER_STEP
    nc = seq // tc
    intra, q_dec, k_dec, c_dec = _ret_tables(RET_CHUNK)
    row = lambda b, c: b * nc + c
    const3 = lambda b, c: (0, 0, 0)
    return pl.pallas_call(
        _ret_prompt_kernel,
        grid=(batch, nc),
        in_specs=[
            pl.BlockSpec((tc, RET_QK_W), lambda b, c: (row(b, c), OFF_QR // RET_QK_W)),
            pl.BlockSpec((tc, RET_QK_W), lambda b, c: (row(b, c), OFF_KR // RET_QK_W)),
            pl.BlockSpec((tc, RET_V_W), lambda b, c: (row(b, c), OFF_VR // RET_V_W)),
            pl.BlockSpec((tc, RET_V_W), lambda b, c: (row(b, c), OFF_GR // RET_V_W)),
            pl.BlockSpec(intra.shape, const3),
            pl.BlockSpec(q_dec.shape, const3),
            pl.BlockSpec(k_dec.shape, const3),
            pl.BlockSpec(c_dec.shape, const3),
            pl.BlockSpec((RET_HEADS, 1, RET_DV), const3),
        ],
        out_specs=[
            pl.BlockSpec((tc, RET_V_W), lambda b, c: (row(b, c), 0)),
            pl.BlockSpec((1, RET_HEADS, RET_DK, RET_DV), lambda b, c: (b, 0, 0, 0)),
        ],
        out_shape=[
            jax.ShapeDtypeStruct((batch * seq, RET_V_W), BF16),
            jax.ShapeDtypeStruct((batch, RET_HEADS, RET_DK, RET_DV), F32),
        ],
        scratch_shapes=[pltpu.VMEM((RET_HEADS, RET_DK, RET_DV), F32)],
        compiler_params=_cparams(("parallel", "arbitrary")),
        name="ret_prompt",
    )(zret, zret, zret, zret, intra, q_dec, k_dec, c_dec, ret_gn)


def _ret_sample_kernel(q_ref, k_ref, v_ref, g_ref, s_ref, intra_ref, qdec_ref, kdec_ref, cdec_ref, gn_ref,
                       o_ref, snew_ref, *, t):
    for b in range(RET_SAMPLE_BB):
        rows = slice(b * t, (b + 1) * t)
        for h in range(RET_HEADS):
            ks = slice(h * RET_DK, (h + 1) * RET_DK)
            vs = slice(h * RET_DV, (h + 1) * RET_DV)
            y, s_new = _ret_head(q_ref[rows, ks], k_ref[rows, ks], v_ref[rows, vs], g_ref[rows, vs], s_ref[b, h],
                                 intra_ref[h], qdec_ref[h], kdec_ref[h], cdec_ref[h], gn_ref[h])
            snew_ref[b, h] = s_new
            o_ref[rows, vs] = y.astype(BF16)


def _ret_sample(zret, state, ret_gn, batch, t):
    bb = RET_SAMPLE_BB
    tr = bb * t
    intra, q_dec, k_dec, c_dec = _ret_tables(t)
    const3 = lambda i: (0, 0, 0)
    st_spec = pl.BlockSpec((bb, RET_HEADS, RET_DK, RET_DV), lambda i: (i, 0, 0, 0))
    return pl.pallas_call(
        functools.partial(_ret_sample_kernel, t=t),
        grid=(batch // bb,),
        in_specs=[
            pl.BlockSpec((tr, RET_QK_W), lambda i: (i, OFF_QR // RET_QK_W)),
            pl.BlockSpec((tr, RET_QK_W), lambda i: (i, OFF_KR // RET_QK_W)),
            pl.BlockSpec((tr, RET_V_W), lambda i: (i, OFF_VR // RET_V_W)),
            pl.BlockSpec((tr, RET_V_W), lambda i: (i, OFF_GR // RET_V_W)),
            st_spec,
            pl.BlockSpec(intra.shape, const3),
            pl.BlockSpec(q_dec.shape, const3),
            pl.BlockSpec(k_dec.shape, const3),
            pl.BlockSpec(c_dec.shape, const3),
            pl.BlockSpec((RET_HEADS, 1, RET_DV), const3),
        ],
        out_specs=[pl.BlockSpec((tr, RET_V_W), lambda i: (i, 0)), st_spec],
        out_shape=[
            jax.ShapeDtypeStruct((batch * t, RET_V_W), BF16),
            jax.ShapeDtypeStruct((batch, RET_HEADS, RET_DK, RET_DV), F32),
        ],
        compiler_params=_cparams(("parallel",)),
        name="ret_sample",
    )(zret, zret, zret, zret, state, intra, q_dec, k_dec, c_dec, ret_gn)


def _alibi_slopes():
    return jnp.exp2(-8.0 * (jnp.arange(ATT_HEADS, dtype=F32) + 1.0) / ATT_HEADS)


def _softmax_parts(s):
    m = jnp.max(s, axis=-1, keepdims=True)
    e = jnp.exp(s - m)
    den = jnp.sum(e, axis=-1, keepdims=True)
    return e / den, m + jnp.log(den)


def _attn_prompt_kernel(q_ref, kp_ref, kc_ref, vp_ref, vc_ref, tbl_ref, o_ref, lse_ref, *, qb, nres):
    i = pl.program_id(3)
    lane_lo = lax.broadcasted_iota(I32, (1, LANES), 1) < ATT_DH
    kcol = lax.broadcasted_iota(I32, (1, 2 * BAND), 1)
    first = jnp.where((i == 0) & (kcol < BAND), NEG_INF, 0.0)
    for rr in range(nres):
        lanes = slice(rr * LANES, (rr + 1) * LANES)
        k_all = jnp.concatenate([kp_ref[0, :, lanes], kc_ref[0, :, lanes]], axis=0).astype(BF16)
        v_all = jnp.concatenate([vp_ref[0, :, lanes], vc_ref[0, :, lanes]], axis=0).astype(BF16)
        for j in range(qb):
            rows = slice(j * BAND, (j + 1) * BAND)
            q = q_ref[0, rows, lanes]
            k = k_all[j * BAND:(j + 2) * BAND]
            v = v_all[j * BAND:(j + 2) * BAND]
            outs, lses = [], []
            for hh in range(2):
                keep = lane_lo if hh == 0 else jnp.logical_not(lane_lo)
                qm = jnp.where(keep, q, 0.0).astype(BF16)
                s = _dot_nt(qm, k) + tbl_ref[0, hh]
                if j == 0:
                    s = s + first
                p, lse = _softmax_parts(s)
                outs.append(_dot(p.astype(BF16), v))
                lses.append(lse)
            o_ref[0, rows, lanes] = jnp.where(lane_lo, outs[0], outs[1])
            lse_ref[0, rows, lanes] = jnp.where(lane_lo, lses[0], lses[1])


def _attn_prompt(qa, ka, va, group, batch, seq):
    win, dil = ATT_GROUPS[group]
    n_back = win // dil
    nb = seq // dil // BAND
    qb = min(ATTN_Q_BLOCKS, nb)
    nbq = nb // qb
    nres = min(dil, ATTN_Q_BLOCKS // qb)
    slopes = _alibi_slopes()[group * ATT_HPG:(group + 1) * ATT_HPG]
    qi = jnp.arange(BAND)[:, None]
    kc = jnp.arange(2 * BAND)[None, :]
    j = qi - kc + BAND
    valid = (j >= 0) & (j <= n_back)
    tbl = jnp.where(valid[None], -slopes[:, None, None] * (dil * j).astype(F32)[None], NEG_INF)
    tbl = tbl.reshape(SLABS_PER_GROUP, 2, BAND, 2 * BAND)

    cur = lambda s, b, r, i: (s, b * nbq + i, r)
    prev = lambda s, b, r, i: (s, b * nb + jnp.maximum(i * qb - 1, 0), r)
    cur_blk = pl.BlockSpec((1, qb * BAND, nres * LANES), cur)
    prev_blk = pl.BlockSpec((1, BAND, nres * LANES), prev)
    out = jax.ShapeDtypeStruct(qa.shape, F32)
    return pl.pallas_call(
        functools.partial(_attn_prompt_kernel, qb=qb, nres=nres),
        grid=(SLABS_PER_GROUP, batch, dil // nres, nbq),
        in_specs=[
            cur_blk, prev_blk, cur_blk, prev_blk, cur_blk,
            pl.BlockSpec((1, 2, BAND, 2 * BAND), lambda s, b, r, i: (s, 0, 0, 0)),
        ],
        out_specs=[cur_blk, cur_blk],
        out_shape=[out, out],
        compiler_params=_cparams(("parallel", "parallel", "parallel", "parallel")),
        name=f"attn_prompt_g{group}",
    )(qa, ka, ka, va, va, tbl)


def _shift_cache(cache_ref, newc_ref, b, tail, t):
    ncol = cache_ref.shape[-1] // LANES
    keep_lanes = lax.broadcasted_iota(I32, (1, LANES), 1) < LANES - t
    prev_rot = None
    for c in range(ncol):
        rot = pltpu.roll(cache_ref[b, :, c * LANES:(c + 1) * LANES], LANES - t, axis=1)
        if c > 0:
            newc_ref[b, :, (c - 1) * LANES:c * LANES] = jnp.where(keep_lanes, prev_rot, rot)
        prev_rot = rot
    newc_ref[b, :, (ncol - 1) * LANES:ncol * LANES] = jnp.where(keep_lanes, prev_rot, tail)


def _attn_sample_kernel(cache_ref, q_ref, kn_ref, vn_ref, tblc_ref, tbln_ref, newc_ref, o_ref, lse_ref, *, wc, t, bb):
    kw = ATT_OUT_W
    rows = ATT_HPG * t
    rhead = lax.broadcasted_iota(I32, (rows, kw), 0) // t
    lhead = lax.broadcasted_iota(I32, (rows, kw), 1) // ATT_DH
    hm = rhead == lhead
    for b in range(bb):
        tok = slice(b * t, (b + 1) * t)
        kn = jnp.concatenate([kn_ref[0, tok], kn_ref[1, tok]], axis=1)
        vn = jnp.concatenate([vn_ref[0, tok], vn_ref[1, tok]], axis=1)
        pad = jnp.zeros((BAND - t, kw), F32)
        knp = jnp.concatenate([kn, pad], axis=0)
        vnp = jnp.concatenate([vn, pad], axis=0)

        new_t = jnp.concatenate([knp.T, vnp.T], axis=0)
        tail = pltpu.roll(new_t, LANES - t, axis=1)
        _shift_cache(cache_ref, newc_ref, b, tail, t)

        q = jnp.concatenate([q_ref[0, tok], q_ref[1, tok]], axis=1)
        qb = jnp.where(hm, jnp.concatenate([q] * ATT_HPG, axis=0), 0.0).astype(BF16)
        kt = cache_ref[b, 0:kw, :].astype(BF16)
        vt = cache_ref[b, kw:2 * kw, :].astype(BF16)
        s_c = _dot(qb, kt) + tblc_ref[...]
        s_n = _dot_nt(qb, knp.astype(BF16)) + tbln_ref[...]
        m = jnp.maximum(jnp.max(s_c, axis=-1, keepdims=True), jnp.max(s_n, axis=-1, keepdims=True))
        e_c = jnp.exp(s_c - m)
        e_n = jnp.exp(s_n - m)
        den = jnp.sum(e_c, axis=-1, keepdims=True) + jnp.sum(e_n, axis=-1, keepdims=True)
        o = _dot_nt((e_c / den).astype(BF16), vt) + _dot((e_n / den).astype(BF16), vnp.astype(BF16))
        lse = m + jnp.log(den)
        o = jnp.where(hm, o, 0.0)
        lse = jnp.where(hm, lse, 0.0)
        o_sel = o[0:t]
        lse_sel = lse[0:t]
        for h in range(1, ATT_HPG):
            o_sel = o_sel + o[h * t:(h + 1) * t]
            lse_sel = lse_sel + lse[h * t:(h + 1) * t]
        for s in range(SLABS_PER_GROUP):
            o_ref[s, tok, :] = o_sel[:, s * LANES:(s + 1) * LANES]
            lse_ref[s, tok, :] = lse_sel[:, s * LANES:(s + 1) * LANES]


def _cache_view(cache):
    batch, wc = cache.shape[:2]
    return jnp.transpose(cache, (0, 2, 3, 4, 1)).reshape(batch, 2 * ATT_OUT_W, wc)


def _cache_unview(cache_t):
    batch, _, wc = cache_t.shape
    return jnp.transpose(cache_t.reshape(batch, 2, ATT_HPG, ATT_DH, wc), (0, 4, 1, 2, 3))


def _attn_sample(qa, ka, va, cache, group, batch, t):
    win, dil = ATT_GROUPS[group]
    n_back = win // dil
    wc = cache.shape[1]
    assert wc == win and wc % LANES == 0 and t % SUBLANES == 0 and t <= BAND
    kw = ATT_OUT_W
    bb = max(1, min(ATTN_SAMPLE_BLOCK_BYTES // (2 * kw * wc * 4), ATTN_SAMPLE_MAX_BB, batch))
    assert batch % bb == 0
    slopes = _alibi_slopes()[group * ATT_HPG:(group + 1) * ATT_HPG]
    qi = jnp.arange(t)[:, None]
    idx = jnp.concatenate([jnp.arange(wc), wc + jnp.arange(BAND)])[None, :]
    dist = wc + qi - idx
    valid = (dist >= 0) & (dist % dil == 0) & (dist <= dil * n_back) & (idx < wc + t)
    tbl = jnp.where(valid[None], -slopes[:, None, None] * dist.astype(F32)[None], NEG_INF)
    tbl = tbl.reshape(ATT_HPG * t, wc + BAND)
    tblc, tbln = tbl[:, :wc], tbl[:, wc:]

    s0 = group
    new_spec = pl.BlockSpec((SLABS_PER_GROUP, bb * t, LANES), lambda b: (s0, b, 0))
    out_spec = pl.BlockSpec((SLABS_PER_GROUP, bb * t, LANES), lambda b: (0, b, 0))
    cache_spec = pl.BlockSpec((bb, 2 * kw, wc), lambda b: (b, 0, 0))
    out = jax.ShapeDtypeStruct((SLABS_PER_GROUP, batch * t, LANES), F32)
    newc, o, lse = pl.pallas_call(
        functools.partial(_attn_sample_kernel, wc=wc, t=t, bb=bb),
        grid=(batch // bb,),
        in_specs=[
            cache_spec, new_spec, new_spec, new_spec,
            pl.BlockSpec(tblc.shape, lambda b: (0, 0)),
            pl.BlockSpec(tbln.shape, lambda b: (0, 0)),
        ],
        out_specs=[cache_spec, out_spec, out_spec],
        out_shape=[jax.ShapeDtypeStruct((batch, 2 * kw, wc), F32), out, out],
        compiler_params=_cparams(("parallel",)),
        name=f"attn_sample_g{group}",
    )(_cache_view(cache), qa, ka, va, tblc, tbln)
    return o, lse, _cache_unview(newc)


def _merge_kernel(x_ref, oret_ref, gate_ref, o0_ref, o1_ref, o2_ref, l0_ref, l1_ref, l2_ref,
                  wa_ref, wb_ref, wo_ref, x1_ref, *scratch, dilated, tm):
    def rows(ref, g, s, kind):
        dil = ATT_GROUPS[g][1]
        if not dilated or dil == 1:
            return ref[s]
        scr = scratch[0]
        slot = ((g - 1) * SLABS_PER_GROUP + s) * 2 + kind
        for r in range(dil):
            scr[slot, pl.ds(r, tm // dil, stride=dil), :] = ref[s, :, r * LANES:(r + 1) * LANES]
        return scr[slot]

    slabs = []
    for s in range(SLABS_PER_GROUP):
        ls = [rows(ref, g, s, 0) for g, ref in enumerate((l0_ref, l1_ref, l2_ref))]
        os_ = [rows(ref, g, s, 1) for g, ref in enumerate((o0_ref, o1_ref, o2_ref))]
        m = jnp.maximum(jnp.maximum(ls[0], ls[1]), ls[2])
        es = [jnp.exp(l - m) for l in ls]
        den = es[0] + es[1] + es[2]
        slabs.append((es[0] / den) * os_[0] + (es[1] / den) * os_[1] + (es[2] / den) * os_[2])
    o_att = jnp.concatenate(slabs, axis=1).astype(BF16)
    ga = gate_ref[:, 0:D_MODEL].astype(F32)
    gb = gate_ref[:, D_MODEL:GATE_W].astype(F32)
    merged = _sigmoid(ga) * _dot(oret_ref[...], wa_ref[...]) + _sigmoid(gb) * _dot(o_att, wb_ref[...])
    x1_ref[...] = x_ref[...] + _dot(merged.astype(BF16), wo_ref[...])


def _merge(x, oret, zgate, os_, ls, wa, wb, wo, dilated):
    n = x.shape[0]
    tm = min(TM_MERGE, n)
    row = lambda w: pl.BlockSpec((tm, w), lambda i: (i, 0))
    full = lambda a: pl.BlockSpec(a.shape, lambda i: (0, 0))

    def slab(g):
        dil = ATT_GROUPS[g][1] if dilated else 1
        return pl.BlockSpec((SLABS_PER_GROUP, tm // dil, dil * LANES), lambda i: (0, i, 0))

    slabs = [slab(g) for g in range(len(ATT_GROUPS))]
    n_scr = (len(ATT_GROUPS) - 1) * SLABS_PER_GROUP * 2
    return pl.pallas_call(
        functools.partial(_merge_kernel, dilated=dilated, tm=tm),
        grid=(n // tm,),
        in_specs=[row(D_MODEL), row(RET_V_W), row(GATE_W)] + slabs + slabs + [full(wa), full(wb), full(wo)],
        out_specs=row(D_MODEL),
        out_shape=jax.ShapeDtypeStruct((n, D_MODEL), F32),
        scratch_shapes=[pltpu.VMEM((n_scr, tm, LANES), F32)] if dilated else [],
        compiler_params=_cparams(("parallel",)),
        name="merge_dilated" if dilated else "merge",
    )(x, oret, zgate, *os_, *ls, wa, wb, wo)


def _router_kernel(xp_ref, xs_ref, g_ref, wr_ref, br_ref, tri_ref, h_ref, idx_ref, rank_ref, w_ref, cnt_ref,
                   run_scr, *, n_prompt_tiles, tm):
    i = pl.program_id(0)

    @pl.when(i == 0)
    def _():
        run_scr[...] = jnp.zeros_like(run_scr)

    x = jnp.where(i < n_prompt_tiles, xp_ref[...], xs_ref[...])
    h = _rms(x, g_ref[...])
    for j in range(ROW_TILES):
        h_ref[pl.ds(j, tm, stride=ROW_TILES), :] = h[:, j * LANES:(j + 1) * LANES]

    logits = jnp.dot(h, wr_ref[...], precision=lax.Precision.HIGHEST, preferred_element_type=F32) + br_ref[...]
    lanes = lax.broadcasted_iota(I32, (1, LANES), 1)
    lanes_f = lanes.astype(F32)
    work = logits
    vals, idxs, hots = [], [], []
    for _ in range(TOP_K):
        m = jnp.max(work, axis=-1, keepdims=True)
        idx = jnp.min(jnp.where(work == m, lanes_f, float(LANES)), axis=-1, keepdims=True)
        hot = lanes_f == idx
        vals.append(m)
        idxs.append(idx)
        hots.append(hot)
        work = jnp.where(hot, -jnp.inf, work)
    exps = [jnp.exp(v - vals[0]) for v in vals]
    den = exps[0] + exps[1] + exps[2] + exps[3]

    cat = jnp.concatenate([hot.astype(BF16) for hot in hots], axis=1)
    cum = _dot(tri_ref[...], cat)
    prev = run_scr[...]
    idx_out = jnp.zeros((tm, LANES), I32)
    rank_out = jnp.zeros((tm, LANES), I32)
    w_out = jnp.zeros((tm, LANES), F32)
    for k in range(TOP_K):
        hot_f = hots[k].astype(F32)
        rank = jnp.sum(hot_f * (cum[:, k * LANES:(k + 1) * LANES] + prev), axis=-1, keepdims=True)
        prev = prev + jnp.sum(hot_f, axis=0, keepdims=True)
        idx_out = jnp.where(lanes == k, idxs[k].astype(I32), idx_out)
        rank_out = jnp.where(lanes == k, rank.astype(I32), rank_out)
        w_out = jnp.where(lanes == k, exps[k] / den, w_out)
    run_scr[...] = prev
    idx_ref[...] = idx_out
    rank_ref[...] = rank_out
    w_ref[...] = w_out
    cnt_ref[...] = prev.astype(I32)


def _router(x1p, x1s, g_moe, wr_pad, br_pad):
    tm = TM_ROUTE
    npt, nst = x1p.shape[0] // tm, x1s.shape[0] // tm
    n = x1p.shape[0] + x1s.shape[0]
    tri = (jnp.arange(tm)[:, None] > jnp.arange(tm)[None, :]).astype(BF16)
    row128 = pl.BlockSpec((tm, LANES), lambda i: (i, 0))
    return pl.pallas_call(
        functools.partial(_router_kernel, n_prompt_tiles=npt, tm=tm),
        grid=(npt + nst,),
        in_specs=[
            pl.BlockSpec((tm, D_MODEL), lambda i: (jnp.minimum(i, npt - 1), 0)),
            pl.BlockSpec((tm, D_MODEL), lambda i: (jnp.clip(i - npt, 0, nst - 1), 0)),
            pl.BlockSpec((1, D_MODEL), lambda i: (0, 0)),
            pl.BlockSpec((D_MODEL, LANES), lambda i: (0, 0)),
            pl.BlockSpec((1, LANES), lambda i: (0, 0)),
            pl.BlockSpec((tm, tm), lambda i: (0, 0)),
        ],
        out_specs=[
            pl.BlockSpec((tm * ROW_TILES, LANES), lambda i: (i, 0)),
            row128, row128, row128,
            pl.BlockSpec((1, LANES), lambda i: (0, 0)),
        ],
        out_shape=[
            jax.ShapeDtypeStruct((n * ROW_TILES, LANES), F32),
            jax.ShapeDtypeStruct((n, LANES), I32),
            jax.ShapeDtypeStruct((n, LANES), I32),
            jax.ShapeDtypeStruct((n, LANES), F32),
            jax.ShapeDtypeStruct((1, LANES), I32),
        ],
        scratch_shapes=[pltpu.VMEM((1, LANES), F32)],
        compiler_params=_cparams(("arbitrary",)),
        name="router",
    )(x1p, x1s, g_moe, wr_pad, br_pad, tri)


def _row_copy(src_ref, src_row, dst_ref, dst_row, sem):
    return pltpu.make_async_copy(
        src_ref.at[pl.ds(pl.multiple_of(src_row * ROW_TILES, ROW_TILES), ROW_TILES), :],
        dst_ref.at[pl.ds(pl.multiple_of(dst_row * ROW_TILES, ROW_TILES), ROW_TILES), :],
        sem)


def _dispatch_kernel(zstart_ref, zlen_ref, nused_ref, slots_ref, h_ref, hs_ref, zero_scr, sem, *, tm, n_tiles):
    i = pl.program_id(0)

    def zero_copy(row, n_rows):
        start = pl.multiple_of(row * ROW_TILES, ROW_TILES)
        return pltpu.make_async_copy(zero_scr.at[pl.ds(0, n_rows * ROW_TILES), :],
                                     hs_ref.at[pl.ds(start, n_rows * ROW_TILES), :], sem)

    def pad_copies(e, op):
        row, left = zstart_ref[e], zlen_ref[e]
        for b in reversed(range(TM_MOE.bit_length() - 1)):
            bit = (left >> b) & 1

            @pl.when(bit == 1)
            def _():
                op(zero_copy(row, 1 << b))

            row = row + (bit << b)

    def all_copies(op):
        lax.fori_loop(0, N_EXPERTS, lambda e, c: (pad_copies(e, op), c)[1], 0)
        lax.fori_loop(nused_ref[0], n_tiles, lambda tt, c: (op(zero_copy(tt * TM_MOE, TM_MOE)), c)[1], 0)

    @pl.when(i == 0)
    def _():
        zero_scr[...] = jnp.zeros_like(zero_scr)
        all_copies(lambda c: c.start())
        all_copies(lambda c: c.wait())

    def body(tok, carry):
        for k in range(TOP_K):
            _row_copy(h_ref, tok, hs_ref, slots_ref[0, 0, tok * TOP_K + k], sem).start(priority=k % 2)
        return carry

    lax.fori_loop(0, tm, body, 0, unroll=8)
    for _ in range(TOP_K):
        pltpu.make_async_copy(h_ref, hs_ref.at[pl.ds(0, tm * ROW_TILES), :], sem).wait()


def _dispatch(hrow, slots, zstart, zlen, n_used, n_tiles):
    tm = TM_ROUTE
    n = hrow.shape[0] // ROW_TILES
    nt = n // tm
    slots3 = slots.reshape(nt, 1, tm * TOP_K)
    grid_spec = pltpu.PrefetchScalarGridSpec(
        num_scalar_prefetch=3,
        grid=(nt,),
        in_specs=[
            pl.BlockSpec((1, 1, tm * TOP_K), lambda i, z, l, u: (i, 0, 0), memory_space=pltpu.SMEM),
            pl.BlockSpec((tm * ROW_TILES, LANES), lambda i, z, l, u: (i, 0)),
        ],
        out_specs=pl.BlockSpec(memory_space=pl.ANY),
        scratch_shapes=[pltpu.VMEM((TM_MOE * ROW_TILES, LANES), F32), pltpu.SemaphoreType.DMA(())],
    )
    return pl.pallas_call(
        functools.partial(_dispatch_kernel, tm=tm, n_tiles=n_tiles),
        grid_spec=grid_spec,
        out_shape=jax.ShapeDtypeStruct((n_tiles * TM_MOE * ROW_TILES, LANES), F32),
        compiler_params=_cparams(("arbitrary",)),
        name="dispatch",
    )(zstart, zlen, n_used, slots3, hrow)


def _experts_kernel(texp_ref, nused_ref, hs_ref, wu_ref, bu_ref, wd_ref, bd_ref, ys_ref, wu_scr, wd_scr):
    j = pl.program_id(0)
    tm = TM_MOE
    n_used = nused_ref[0]
    jj = jnp.minimum(j, n_used - 1)
    new_expert = (j == 0) | (texp_ref[jj] != texp_ref[jnp.maximum(jj - 1, 0)])

    @pl.when((j < n_used) & new_expert)
    def _():
        wu_scr[...] = wu_ref[0].astype(BF16)
        wd_scr[...] = wd_ref[0].astype(BF16)

    @pl.when(j < n_used)
    def _():
        x = jnp.concatenate(
            [hs_ref[pl.ds(c, tm, stride=ROW_TILES), :] for c in range(ROW_TILES)], axis=1).astype(BF16)
        u = _dot(x, wu_scr[...]) + bu_ref[0]
        glu = jnp.minimum(u[:, :D_EXPERT], SWIGLU_LIMIT)
        lin = jnp.clip(u[:, D_EXPERT:], -SWIGLU_LIMIT, SWIGLU_LIMIT)
        a = glu * _sigmoid(SWIGLU_ALPHA * glu) * (lin + 1.0)
        y = _dot(a.astype(BF16), wd_scr[...]) + bd_ref[0]
        for c in range(ROW_TILES):
            ys_ref[pl.ds(c, tm, stride=ROW_TILES), :] = y[:, c * LANES:(c + 1) * LANES]

    @pl.when(j >= n_used)
    def _():
        ys_ref[...] = jnp.zeros_like(ys_ref)


def _experts(hs, tile_expert, n_used, w_up, b_up, w_down, b_down, n_tiles):
    tm = TM_MOE
    tile = lambda j, te, nu: (jnp.minimum(j, nu[0] - 1), 0)
    exp3 = lambda j, te, nu: (te[jnp.minimum(j, nu[0] - 1)], 0, 0)
    grid_spec = pltpu.PrefetchScalarGridSpec(
        num_scalar_prefetch=2,
        grid=(n_tiles,),
        in_specs=[
            pl.BlockSpec((tm * ROW_TILES, LANES), tile),
            pl.BlockSpec((1, D_MODEL, 2 * D_EXPERT), exp3),
            pl.BlockSpec((1, 1, 2 * D_EXPERT), exp3),
            pl.BlockSpec((1, D_EXPERT, D_MODEL), exp3),
            pl.BlockSpec((1, 1, D_MODEL), exp3),
        ],
        out_specs=pl.BlockSpec((tm * ROW_TILES, LANES), lambda j, te, nu: (j, 0)),
        scratch_shapes=[pltpu.VMEM((D_MODEL, 2 * D_EXPERT), BF16), pltpu.VMEM((D_EXPERT, D_MODEL), BF16)],
    )
    return pl.pallas_call(
        _experts_kernel,
        grid_spec=grid_spec,
        out_shape=jax.ShapeDtypeStruct((n_tiles * tm * ROW_TILES, LANES), F32),
        compiler_params=_cparams(("arbitrary",)),
        name="experts",
    )(tile_expert, n_used, hs, w_up, b_up.reshape(N_EXPERTS, 1, -1), w_down, b_down.reshape(N_EXPERTS, 1, -1))


def _combine_kernel(slots_ref, nslots_ref, x1_ref, w_ref, p_ref, g_ref, wpg_ref, wpe_ref, ys_ref, y_ref,
                    buf, x2_scr, sems, *, tm):
    i = pl.program_id(0)
    cur = lax.rem(i, 2)
    nxt = 1 - cur

    def gather(sref, half, tok, k):
        return _row_copy(ys_ref, sref[0, 0, tok * TOP_K + k], buf.at[half, k], tok, sems.at[half])

    def wait_tile(half):
        for k in range(TOP_K):
            pltpu.make_async_copy(ys_ref.at[pl.ds(0, tm * ROW_TILES), :], buf.at[half, k], sems.at[half]).wait()

    @pl.when(i == 0)
    def _():
        def body(tok, carry):
            for k in range(TOP_K):
                gather(slots_ref, 0, tok, k).start(priority=k % 2)
            return carry

        lax.fori_loop(0, tm, body, 0)

    wait_tile(cur)
    rg = COMB_ROW_GROUP
    per = tm // (tm // rg * ROW_TILES)
    for r in range(tm // rg):
        w = w_ref[r * rg:(r + 1) * rg, :]
        wk = [jnp.broadcast_to(w[:, k:k + 1], (rg, LANES)) for k in range(TOP_K)]
        for c in range(ROW_TILES):
            acc = x1_ref[r * rg:(r + 1) * rg, c * LANES:(c + 1) * LANES]
            for k in range(TOP_K):
                acc = acc + wk[k] * buf[cur, k, pl.ds(r * rg * ROW_TILES + c, rg, stride=ROW_TILES), :]
            g = r * ROW_TILES + c
            for tok in range(g * per, (g + 1) * per):
                for k in range(TOP_K):
                    gather(nslots_ref, nxt, tok, k).start(priority=k % 2)
            x2_scr[r * rg:(r + 1) * rg, c * LANES:(c + 1) * LANES] = acc
    x2 = x2_scr[...]
    gate = _sigmoid(_dot(_rms(x2, g_ref[...]).astype(BF16), wpg_ref[...]))
    y_ref[...] = x2 + gate * _dot(p_ref[...].astype(BF16), wpe_ref[...])

    @pl.when(i == pl.num_programs(0) - 1)
    def _():
        wait_tile(nxt)


def _combine(x1, slots, topw, p, g_ple, wpg, wpe, ys):
    n = x1.shape[0]
    tm = TM_COMB
    nt = n // tm
    slots3 = slots.reshape(nt, 1, tm * TOP_K)
    full = lambda a: pl.BlockSpec(a.shape, lambda i: (0, 0))
    return pl.pallas_call(
        functools.partial(_combine_kernel, tm=tm),
        grid=(nt,),
        in_specs=[
            pl.BlockSpec((1, 1, tm * TOP_K), lambda i: (i, 0, 0), memory_space=pltpu.SMEM),
            pl.BlockSpec((1, 1, tm * TOP_K), lambda i: (jnp.minimum(i + 1, nt - 1), 0, 0), memory_space=pltpu.SMEM),
            pl.BlockSpec((tm, D_MODEL), lambda i: (i, 0)),
            pl.BlockSpec((tm, LANES), lambda i: (i, 0)),
            pl.BlockSpec((tm, PLE_DIM), lambda i: (i, 0)),
            full(g_ple), full(wpg), full(wpe),
            pl.BlockSpec(memory_space=pl.ANY),
        ],
        out_specs=pl.BlockSpec((tm, D_MODEL), lambda i: (i, 0)),
        out_shape=jax.ShapeDtypeStruct((n, D_MODEL), F32),
        scratch_shapes=[pltpu.VMEM((2, TOP_K, tm * ROW_TILES, LANES), F32), pltpu.VMEM((tm, D_MODEL), F32),
                        pltpu.SemaphoreType.DMA((2,))],
        compiler_params=_cparams(("arbitrary",)),
        name="combine",
    )(slots3, slots3, x1, topw, p, g_ple, wpg, wpe, ys)


def _prompt_cache_kernel(k_ref, v_ref, o_ref, *scratch, dil, keep):
    for kv, ref in enumerate((k_ref, v_ref)):
        for s in range(SLABS_PER_GROUP):
            if dil == 1:
                tok = ref[s]
            else:
                scr = scratch[0]
                for r in range(dil):
                    scr[pl.ds(r, keep // dil, stride=dil), :] = ref[s, :, r * LANES:(r + 1) * LANES]
                tok = scr[...]
            row0 = kv * ATT_OUT_W + s * LANES
            o_ref[0, row0:row0 + LANES, :] = tok.T


def _cache_from_slabs(ka, va, group, batch, seq, keep):
    dil = ATT_GROUPS[group][1]
    assert keep % dil == 0 and seq % keep == 0 and keep % LANES == 0
    blocks_per_seq = seq // keep
    tail = pl.BlockSpec((SLABS_PER_GROUP, keep // dil, dil * LANES), lambda b: (0, (b + 1) * blocks_per_seq - 1, 0))
    cache_t = pl.pallas_call(
        functools.partial(_prompt_cache_kernel, dil=dil, keep=keep),
        grid=(batch,),
        in_specs=[tail, tail],
        out_specs=pl.BlockSpec((1, 2 * ATT_OUT_W, keep), lambda b: (b, 0, 0)),
        out_shape=jax.ShapeDtypeStruct((batch, 2 * ATT_OUT_W, keep), F32),
        scratch_shapes=[pltpu.VMEM((keep, LANES), F32)] if dil > 1 else [],
        compiler_params=_cparams(("parallel",)),
        name=f"prompt_cache_g{group}",
    )(ka, va)
    return _cache_unview(cache_t)


def _layer(xp, xs, pp, ps, state, caches, g_mix, w_in, ret_gn, qn_g, kn_g, w_a, w_b, w_o, g_moe, w_router, b_router,
           w_up, b_up, w_down, b_down, g_ple, w_ple_gate, w_ple):
    batch, seq, _ = xp.shape
    dbatch, dseq, _ = xs.shape
    n_p, n_s = batch * seq, dbatch * dseq
    n = n_p + n_s

    w_in_bf = w_in.astype(BF16)
    g_mix2 = g_mix.reshape(1, D_MODEL)
    qg = qn_g.reshape(N_SLABS, 1, LANES)
    kg = kn_g.reshape(N_SLABS, 1, LANES)
    gn = ret_gn.reshape(RET_HEADS, 1, RET_DV)
    wa, wb, wo = w_a.astype(BF16), w_b.astype(BF16), w_o.astype(BF16)

    xp2 = xp.reshape(n_p, D_MODEL)
    xs2 = xs.reshape(n_s, D_MODEL)
    zret_p, zgate_p, *qkv_p = _inproj(xp2, g_mix2, w_in_bf, qg, kg, dilated=True)
    zret_s, zgate_s, qa_s, ka_s, va_s = _inproj(xs2, g_mix2, w_in_bf, qg, kg, dilated=False)

    oret_p, state_p = _ret_prompt(zret_p, gn, batch, seq)
    oret_s, state_s = _ret_sample(zret_s, state, gn, dbatch, dseq)

    o_p, l_p, o_s, l_s, win_p, win_s = [], [], [], [], [], []
    for g, (win, _) in enumerate(ATT_GROUPS):
        qa_g, ka_g, va_g = qkv_p[3 * g:3 * g + 3]
        o, l = _attn_prompt(qa_g, ka_g, va_g, g, batch, seq)
        o_p.append(o)
        l_p.append(l)
        win_p.append(_cache_from_slabs(ka_g, va_g, g, batch, seq, min(win, seq)))
        o, l, newc = _attn_sample(qa_s, ka_s, va_s, caches[g], g, dbatch, dseq)
        o_s.append(o)
        l_s.append(l)
        win_s.append(newc)

    x1p = _merge(xp2, oret_p, zgate_p, o_p, l_p, wa, wb, wo, dilated=True)
    x1s = _merge(xs2, oret_s, zgate_s, o_s, l_s, wa, wb, wo, dilated=False)

    wr_pad = jnp.zeros((D_MODEL, LANES), F32).at[:, :N_EXPERTS].set(w_router)
    br_pad = jnp.full((1, LANES), NEG_INF, F32).at[0, :N_EXPERTS].set(b_router)
    hrow, top_i, rank, top_w, counts = _router(x1p, x1s, g_moe.reshape(1, D_MODEL), wr_pad, br_pad)

    cnt = counts[0, :N_EXPERTS]
    padded = (cnt + TM_MOE - 1) // TM_MOE * TM_MOE
    ends = jnp.cumsum(padded)
    goff = ends - padded
    slots = goff[top_i[:, :TOP_K].reshape(-1)] + rank[:, :TOP_K].reshape(-1)
    n_tiles = (n * TOP_K) // TM_MOE + N_EXPERTS
    tile_ends = ends // TM_MOE
    n_used = tile_ends[-1:].astype(I32)
    tile_expert = jnp.sum(tile_ends[None, :] <= jnp.arange(n_tiles)[:, None], axis=1)
    tile_expert = jnp.minimum(tile_expert, N_EXPERTS - 1).astype(I32)
    zstart = (goff + cnt).astype(I32)
    zlen = (padded - cnt).astype(I32)

    hs = _dispatch(hrow, slots, zstart, zlen, n_used, n_tiles)
    ys = _experts(hs, tile_expert, n_used, w_up, b_up, w_down, b_down, n_tiles)

    g_ple2 = g_ple.reshape(1, D_MODEL)
    wpg, wpe = w_ple_gate.astype(BF16), w_ple.astype(BF16)
    yp = _combine(x1p, slots[:n_p * TOP_K], top_w[:n_p], pp.reshape(n_p, PLE_DIM), g_ple2, wpg, wpe, ys)
    ys_out = _combine(x1s, slots[n_p * TOP_K:], top_w[n_p:], ps.reshape(n_s, PLE_DIM), g_ple2, wpg, wpe, ys)
    return (yp.reshape(batch, seq, D_MODEL), ys_out.reshape(dbatch, dseq, D_MODEL), state_p, state_s, win_p, win_s)


def kernel(x_prompt, x_sample, p_prompt, p_sample, state_ret, cache_win128_kv, cache_win512_kv, cache_win2048_kv, norm_mix_g, w_in, ret_norm_g, q_norm_g, k_norm_g, w_a, w_b, w_o, norm_moe_g, w_router, b_router, w_up, b_up, w_down, b_down, norm_ple_g, w_ple_gate, w_ple):
    caches = (cache_win128_kv, cache_win512_kv, cache_win2048_kv)
    depth = w_in.shape[0]
    xp, xs = x_prompt, x_sample
    ret_p, ret_s = [], []
    win_p = [[] for _ in ATT_GROUPS]
    win_s = [[] for _ in ATT_GROUPS]
    for i in range(depth):
        xp, xs, sp, ss, bp, bs = _layer(
            xp, xs, p_prompt[i], p_sample[i], state_ret[i], [c[i] for c in caches], norm_mix_g[i], w_in[i],
            ret_norm_g[i], q_norm_g[i], k_norm_g[i], w_a[i], w_b[i], w_o[i], norm_moe_g[i], w_router[i], b_router[i],
            w_up[i], b_up[i], w_down[i], b_down[i], norm_ple_g[i], w_ple_gate[i], w_ple[i])
        ret_p.append(sp)
        ret_s.append(ss)
        for g in range(len(ATT_GROUPS)):
            win_p[g].append(bp[g])
            win_s[g].append(bs[g])
    return (xp, xs, jnp.stack(ret_p), jnp.stack(ret_s), jnp.stack(win_p[0]), jnp.stack(win_s[0]),
            jnp.stack(win_p[1]), jnp.stack(win_s[1]), jnp.stack(win_p[2]), jnp.stack(win_s[2]))
```

```python
import functools

import jax
import jax.numpy as jnp
from jax import lax
from jax.experimental import pallas as pl
from jax.experimental.pallas import tpu as pltpu

F32 = jnp.float32
BF16 = jnp.bfloat16
I32 = jnp.int32

D_MODEL = 1024
RET_HEADS = 4
RET_DK = 128
RET_DV = 256
RET_QK_W = RET_HEADS * RET_DK
RET_V_W = RET_HEADS * RET_DV
RET_CHUNK = 128
ATT_GROUPS = ((128, 1), (512, 4), (2048, 16))
ATT_HPG = 4
ATT_HEADS = ATT_HPG * len(ATT_GROUPS)
ATT_DH = 64
ATT_W = ATT_HEADS * ATT_DH
ATT_OUT_W = ATT_HPG * ATT_DH
BAND = 128
N_SLABS = ATT_W // 128
SLABS_PER_GROUP = ATT_OUT_W // 128
N_EXPERTS = 32
TOP_K = 4
D_EXPERT = 1024
SWIGLU_LIMIT = 7.0
SWIGLU_ALPHA = 1.702
PLE_DIM = 256
EPS = 1e-6
NEG_INF = -1e30

OFF_QR = 0
OFF_KR = OFF_QR + RET_QK_W
OFF_VR = OFF_KR + RET_QK_W
OFF_GR = OFF_VR + RET_V_W
OFF_QA = OFF_GR + RET_V_W
OFF_KA = OFF_QA + ATT_W
OFF_VA = OFF_KA + ATT_W
OFF_GA = OFF_VA + ATT_W
OFF_GB = OFF_GA + D_MODEL
N_IN = OFF_GB + D_MODEL
RET_W = OFF_QA
GATE_W = 2 * D_MODEL

LANES = 128
SUBLANES = 8
ROW_TILES = D_MODEL // LANES
VMEM_LIMIT = 56 * 1024 * 1024

TM_INPROJ = 512
TM_MERGE = 512
TM_ROUTE = 512
TM_MOE = 512
TM_COMB = 256
COMB_ROW_GROUP = 64
RET_CHUNKS_PER_STEP = 8
RET_SAMPLE_BB = 8
ATTN_Q_BLOCKS = 16
ATTN_SAMPLE_BLOCK_BYTES = 8 * 1024 * 1024
ATTN_SAMPLE_MAX_BB = 16


def _cparams(sem):
    return pltpu.CompilerParams(dimension_semantics=sem, vmem_limit_bytes=VMEM_LIMIT)


def _dot(a, b):
    return jnp.dot(a, b, preferred_element_type=F32)


def _dot_nt(a, b):
    return lax.dot_general(a, b, (((1,), (1,)), ((), ())), preferred_element_type=F32)


def _dot_tn(a, b):
    return lax.dot_general(a, b, (((0,), (0,)), ((), ())), preferred_element_type=F32)


def _rms(x, g):
    return x * lax.rsqrt(jnp.mean(x * x, axis=-1, keepdims=True) + EPS) * g


def _sigmoid(x):
    return 1.0 / (1.0 + jnp.exp(-x))


def _dilated_cols(dil):
    return dil * LANES


def _inproj_kernel(x_ref, g_ref, w_ref, qg_ref, kg_ref, zret_ref, zgate_ref, *rest, dilated, tm):
    h = _rms(x_ref[...], g_ref[...]).astype(BF16)

    def mm(lo, hi):
        return _dot(h, w_ref[:, lo:hi])

    zret_ref[:, OFF_QR:OFF_KR] = mm(OFF_QR, OFF_KR).astype(BF16)
    zret_ref[:, OFF_KR:OFF_VR] = (mm(OFF_KR, OFF_VR) * (RET_DK ** -0.5)).astype(BF16)
    zret_ref[:, OFF_VR:OFF_QA] = mm(OFF_VR, OFF_QA).astype(BF16)
    zgate_ref[...] = mm(OFF_GA, N_IN).astype(BF16)

    lane_lo = lax.broadcasted_iota(I32, (1, LANES), 1) < ATT_DH

    def headnorm(x, g):
        x2 = x * x
        lo = jnp.sum(jnp.where(lane_lo, x2, 0.0), axis=-1, keepdims=True)
        hi = jnp.sum(jnp.where(lane_lo, 0.0, x2), axis=-1, keepdims=True)
        ms = jnp.where(lane_lo, lo, hi) * (1.0 / ATT_DH)
        return x * lax.rsqrt(ms + EPS) * g

    q = mm(OFF_QA, OFF_KA)
    k = mm(OFF_KA, OFF_VA)
    v = mm(OFF_VA, OFF_GA)
    for s in range(N_SLABS):
        sl = slice(s * LANES, (s + 1) * LANES)
        vals = (headnorm(q[:, sl], qg_ref[s]) * (ATT_DH ** -0.5), headnorm(k[:, sl], kg_ref[s]), v[:, sl])
        g, sg = divmod(s, SLABS_PER_GROUP)
        dil = ATT_GROUPS[g][1]
        for a, val in enumerate(vals):
            if not dilated:
                rest[a][s] = val
            elif dil == 1:
                rest[3 * g + a][sg] = val
            else:
                scr = rest[-1]
                slot = (s - SLABS_PER_GROUP) * 3 + a
                scr[slot] = val
                for r in range(dil):
                    rest[3 * g + a][sg, :, r * LANES:(r + 1) * LANES] = scr[slot, pl.ds(r, tm // dil, stride=dil), :]


def _inproj(x, g_mix, w_in_bf, qg, kg, dilated):
    n = x.shape[0]
    tm = TM_INPROJ
    if dilated:
        slab_shapes, slab_specs = [], []
        for _, dil in ATT_GROUPS:
            assert tm % (dil * SUBLANES) == 0
            slab_shapes += [jax.ShapeDtypeStruct((SLABS_PER_GROUP, n // dil, _dilated_cols(dil)), F32)] * 3
            slab_specs += [pl.BlockSpec((SLABS_PER_GROUP, tm // dil, _dilated_cols(dil)), lambda i: (0, i, 0))] * 3
        scratch = [pltpu.VMEM(((N_SLABS - SLABS_PER_GROUP) * 3, tm, LANES), F32)]
    else:
        slab_shapes = [jax.ShapeDtypeStruct((N_SLABS, n, LANES), F32)] * 3
        slab_specs = [pl.BlockSpec((N_SLABS, tm, LANES), lambda i: (0, i, 0))] * 3
        scratch = []
    return pl.pallas_call(
        functools.partial(_inproj_kernel, dilated=dilated, tm=tm),
        grid=(n // tm,),
        in_specs=[
            pl.BlockSpec((tm, D_MODEL), lambda i: (i, 0)),
            pl.BlockSpec((1, D_MODEL), lambda i: (0, 0)),
            pl.BlockSpec((D_MODEL, N_IN), lambda i: (0, 0), pipeline_mode=pl.Buffered(1)),
            pl.BlockSpec((N_SLABS, 1, LANES), lambda i: (0, 0, 0)),
            pl.BlockSpec((N_SLABS, 1, LANES), lambda i: (0, 0, 0)),
        ],
        out_specs=[pl.BlockSpec((tm, RET_W), lambda i: (i, 0)), pl.BlockSpec((tm, GATE_W), lambda i: (i, 0))]
        + slab_specs,
        out_shape=[jax.ShapeDtypeStruct((n, RET_W), BF16), jax.ShapeDtypeStruct((n, GATE_W), BF16)] + slab_shapes,
        scratch_shapes=scratch,
        compiler_params=_cparams(("parallel",)),
        name="inproj_dilated" if dilated else "inproj",
    )(x, g_mix, w_in_bf, qg, kg)


def _ret_tables(chunk):
    lg = jnp.log1p(-jnp.exp2(-5.0 - jnp.arange(RET_HEADS, dtype=F32)))
    pos = jnp.arange(chunk, dtype=F32)
    rel = pos[:, None] - pos[None, :]
    intra = jnp.where(rel[None] >= 0, jnp.exp(lg[:, None, None] * jnp.maximum(rel, 0.0)[None]), 0.0)
    q_dec = jnp.exp(lg[:, None] * (pos[None, :] + 1.0))
    k_dec = jnp.exp(lg[:, None] * (chunk - 1.0 - pos[None, :]))
    c_dec = jnp.exp(lg * chunk)
    q_dec = jnp.broadcast_to(q_dec[:, :, None], (RET_HEADS, chunk, RET_DV))
    k_dec = jnp.broadcast_to(k_dec[:, :, None], (RET_HEADS, chunk, RET_DK))
    c_dec = jnp.broadcast_to(c_dec[:, None, None], (RET_HEADS, 1, RET_DV))
    return intra, q_dec, k_dec, c_dec


def _ret_head(q, kf, v, gr, s, intra, q_dec, k_dec, c_dec, gn):
    qb = q.astype(BF16)
    vb = v.astype(BF16)
    gr = gr.astype(F32)
    sc = _dot_nt(qb, kf.astype(BF16)) * intra
    o = _dot(sc.astype(BF16), vb) + _dot(qb, s.astype(BF16)) * q_dec
    s_new = s * c_dec + _dot_tn((kf.astype(F32) * k_dec).astype(BF16), vb)
    y = _rms(o, gn) * (gr * _sigmoid(gr))
    return y, s_new


def _ret_prompt_kernel(q_ref, k_ref, v_ref, g_ref, intra_ref, qdec_ref, kdec_ref, cdec_ref, gn_ref,
                       o_ref, sfin_ref, s_scr):
    c_idx = pl.program_id(1)

    @pl.when(c_idx == 0)
    def _():
        s_scr[...] = jnp.zeros_like(s_scr)

    for c in range(RET_CHUNKS_PER_STEP):
        rows = slice(c * RET_CHUNK, (c + 1) * RET_CHUNK)
        for h in range(RET_HEADS):
            ks = slice(h * RET_DK, (h + 1) * RET_DK)
            vs = slice(h * RET_DV, (h + 1) * RET_DV)
            y, s_new = _ret_head(q_ref[rows, ks], k_ref[rows, ks], v_ref[rows, vs], g_ref[rows, vs], s_scr[h],
                                 intra_ref[h], qdec_ref[h], kdec_ref[h], cdec_ref[h], gn_ref[h])
            s_scr[h] = s_new
            o_ref[rows, vs] = y.astype(BF16)

    @pl.when(c_idx == pl.num_programs(1) - 1)
    def _():
        sfin_ref[0] = s_scr[...]


def _ret_prompt(zret, ret_gn, batch, seq):
    tc = RET_CHUNK * RET_CHUNKS_PER_STEP
    nc = seq // tc
    intra, q_dec, k_dec, c_dec = _ret_tables(RET_CHUNK)
    row = lambda b, c: b * nc + c
    const3 = lambda b, c: (0, 0, 0)
    return pl.pallas_call(
        _ret_prompt_kernel,
        grid=(batch, nc),
        in_specs=[
            pl.BlockSpec((tc, RET_QK_W), lambda b, c: (row(b, c), OFF_QR // RET_QK_W)),
            pl.BlockSpec((tc, RET_QK_W), lambda b, c: (row(b, c), OFF_KR // RET_QK_W)),
            pl.BlockSpec((tc, RET_V_W), lambda b, c: (row(b, c), OFF_VR // RET_V_W)),
            pl.BlockSpec((tc, RET_V_W), lambda b, c: (row(b, c), OFF_GR // RET_V_W)),
            pl.BlockSpec(intra.shape, const3),
            pl.BlockSpec(q_dec.shape, const3),
            pl.BlockSpec(k_dec.shape, const3),
            pl.BlockSpec(c_dec.shape, const3),
            pl.BlockSpec((RET_HEADS, 1, RET_DV), const3),
        ],
        out_specs=[
            pl.BlockSpec((tc, RET_V_W), lambda b, c: (row(b, c), 0)),
            pl.BlockSpec((1, RET_HEADS, RET_DK, RET_DV), lambda b, c: (b, 0, 0, 0)),
        ],
        out_shape=[
            jax.ShapeDtypeStruct((batch * seq, RET_V_W), BF16),
            jax.ShapeDtypeStruct((batch, RET_HEADS, RET_DK, RET_DV), F32),
        ],
        scratch_shapes=[pltpu.VMEM((RET_HEADS, RET_DK, RET_DV), F32)],
        compiler_params=_cparams(("parallel", "arbitrary")),
        name="ret_prompt",
    )(zret, zret, zret, zret, intra, q_dec, k_dec, c_dec, ret_gn)


def _ret_sample_kernel(q_ref, k_ref, v_ref, g_ref, s_ref, intra_ref, qdec_ref, kdec_ref, cdec_ref, gn_ref,
                       o_ref, snew_ref, *, t):
    for b in range(RET_SAMPLE_BB):
        rows = slice(b * t, (b + 1) * t)
        for h in range(RET_HEADS):
            ks = slice(h * RET_DK, (h + 1) * RET_DK)
            vs = slice(h * RET_DV, (h + 1) * RET_DV)
            y, s_new = _ret_head(q_ref[rows, ks], k_ref[rows, ks], v_ref[rows, vs], g_ref[rows, vs], s_ref[b, h],
                                 intra_ref[h], qdec_ref[h], kdec_ref[h], cdec_ref[h], gn_ref[h])
            snew_ref[b, h] = s_new
            o_ref[rows, vs] = y.astype(BF16)


def _ret_sample(zret, state, ret_gn, batch, t):
    bb = RET_SAMPLE_BB
    tr = bb * t
    intra, q_dec, k_dec, c_dec = _ret_tables(t)
    const3 = lambda i: (0, 0, 0)
    st_spec = pl.BlockSpec((bb, RET_HEADS, RET_DK, RET_DV), lambda i: (i, 0, 0, 0))
    return pl.pallas_call(
        functools.partial(_ret_sample_kernel, t=t),
        grid=(batch // bb,),
        in_specs=[
            pl.BlockSpec((tr, RET_QK_W), lambda i: (i, OFF_QR // RET_QK_W)),
            pl.BlockSpec((tr, RET_QK_W), lambda i: (i, OFF_KR // RET_QK_W)),
            pl.BlockSpec((tr, RET_V_W), lambda i: (i, OFF_VR // RET_V_W)),
            pl.BlockSpec((tr, RET_V_W), lambda i: (i, OFF_GR // RET_V_W)),
            st_spec,
            pl.BlockSpec(intra.shape, const3),
            pl.BlockSpec(q_dec.shape, const3),
            pl.BlockSpec(k_dec.shape, const3),
            pl.BlockSpec(c_dec.shape, const3),
            pl.BlockSpec((RET_HEADS, 1, RET_DV), const3),
        ],
        out_specs=[pl.BlockSpec((tr, RET_V_W), lambda i: (i, 0)), st_spec],
        out_shape=[
            jax.ShapeDtypeStruct((batch * t, RET_V_W), BF16),
            jax.ShapeDtypeStruct((batch, RET_HEADS, RET_DK, RET_DV), F32),
        ],
        compiler_params=_cparams(("parallel",)),
        name="ret_sample",
    )(zret, zret, zret, zret, state, intra, q_dec, k_dec, c_dec, ret_gn)


def _alibi_slopes():
    return jnp.exp2(-8.0 * (jnp.arange(ATT_HEADS, dtype=F32) + 1.0) / ATT_HEADS)


def _softmax_parts(s):
    m = jnp.max(s, axis=-1, keepdims=True)
    e = jnp.exp(s - m)
    den = jnp.sum(e, axis=-1, keepdims=True)
    return e / den, m + jnp.log(den)


def _attn_prompt_kernel(q_ref, kp_ref, kc_ref, vp_ref, vc_ref, tbl_ref, o_ref, lse_ref, *, qb, nres):
    i = pl.program_id(3)
    lane_lo = lax.broadcasted_iota(I32, (1, LANES), 1) < ATT_DH
    kcol = lax.broadcasted_iota(I32, (1, 2 * BAND), 1)
    first = jnp.where((i == 0) & (kcol < BAND), NEG_INF, 0.0)
    for rr in range(nres):
        lanes = slice(rr * LANES, (rr + 1) * LANES)
        k_all = jnp.concatenate([kp_ref[0, :, lanes], kc_ref[0, :, lanes]], axis=0).astype(BF16)
        v_all = jnp.concatenate([vp_ref[0, :, lanes], vc_ref[0, :, lanes]], axis=0).astype(BF16)
        for j in range(qb):
            rows = slice(j * BAND, (j + 1) * BAND)
            q = q_ref[0, rows, lanes]
            k = k_all[j * BAND:(j + 2) * BAND]
            v = v_all[j * BAND:(j + 2) * BAND]
            outs, lses = [], []
            for hh in range(2):
                keep = lane_lo if hh == 0 else jnp.logical_not(lane_lo)
                qm = jnp.where(keep, q, 0.0).astype(BF16)
                s = _dot_nt(qm, k) + tbl_ref[0, hh]
                if j == 0:
                    s = s + first
                p, lse = _softmax_parts(s)
                outs.append(_dot(p.astype(BF16), v))
                lses.append(lse)
            o_ref[0, rows, lanes] = jnp.where(lane_lo, outs[0], outs[1])
            lse_ref[0, rows, lanes] = jnp.where(lane_lo, lses[0], lses[1])


def _attn_prompt(qa, ka, va, group, batch, seq):
    win, dil = ATT_GROUPS[group]
    n_back = win // dil
    nb = seq // dil // BAND
    qb = min(ATTN_Q_BLOCKS, nb)
    nbq = nb // qb
    nres = min(dil, ATTN_Q_BLOCKS // qb)
    slopes = _alibi_slopes()[group * ATT_HPG:(group + 1) * ATT_HPG]
    qi = jnp.arange(BAND)[:, None]
    kc = jnp.arange(2 * BAND)[None, :]
    j = qi - kc + BAND
    valid = (j >= 0) & (j <= n_back)
    tbl = jnp.where(valid[None], -slopes[:, None, None] * (dil * j).astype(F32)[None], NEG_INF)
    tbl = tbl.reshape(SLABS_PER_GROUP, 2, BAND, 2 * BAND)

    cur = lambda s, b, r, i: (s, b * nbq + i, r)
    prev = lambda s, b, r, i: (s, b * nb + jnp.maximum(i * qb - 1, 0), r)
    cur_blk = pl.BlockSpec((1, qb * BAND, nres * LANES), cur)
    prev_blk = pl.BlockSpec((1, BAND, nres * LANES), prev)
    out = jax.ShapeDtypeStruct(qa.shape, F32)
    return pl.pallas_call(
        functools.partial(_attn_prompt_kernel, qb=qb, nres=nres),
        grid=(SLABS_PER_GROUP, batch, dil // nres, nbq),
        in_specs=[
            cur_blk, prev_blk, cur_blk, prev_blk, cur_blk,
            pl.BlockSpec((1, 2, BAND, 2 * BAND), lambda s, b, r, i: (s, 0, 0, 0)),
        ],
        out_specs=[cur_blk, cur_blk],
        out_shape=[out, out],
        compiler_params=_cparams(("parallel", "parallel", "parallel", "parallel")),
        name=f"attn_prompt_g{group}",
    )(qa, ka, ka, va, va, tbl)


def _shift_cache(cache_ref, newc_ref, b, tail, t):
    ncol = cache_ref.shape[-1] // LANES
    keep_lanes = lax.broadcasted_iota(I32, (1, LANES), 1) < LANES - t
    prev_rot = None
    for c in range(ncol):
        rot = pltpu.roll(cache_ref[b, :, c * LANES:(c + 1) * LANES], LANES - t, axis=1)
        if c > 0:
            newc_ref[b, :, (c - 1) * LANES:c * LANES] = jnp.where(keep_lanes, prev_rot, rot)
        prev_rot = rot
    newc_ref[b, :, (ncol - 1) * LANES:ncol * LANES] = jnp.where(keep_lanes, prev_rot, tail)


def _attn_sample_kernel(cache_ref, q_ref, kn_ref, vn_ref, tblc_ref, tbln_ref, newc_ref, o_ref, lse_ref, *, wc, t, bb):
    kw = ATT_OUT_W
    rows = ATT_HPG * t
    rhead = lax.broadcasted_iota(I32, (rows, kw), 0) // t
    lhead = lax.broadcasted_iota(I32, (rows, kw), 1) // ATT_DH
    hm = rhead == lhead
    for b in range(bb):
        tok = slice(b * t, (b + 1) * t)
        kn = jnp.concatenate([kn_ref[0, tok], kn_ref[1, tok]], axis=1)
        vn = jnp.concatenate([vn_ref[0, tok], vn_ref[1, tok]], axis=1)
        pad = jnp.zeros((BAND - t, kw), F32)
        knp = jnp.concatenate([kn, pad], axis=0)
        vnp = jnp.concatenate([vn, pad], axis=0)

        new_t = jnp.concatenate([knp.T, vnp.T], axis=0)
        tail = pltpu.roll(new_t, LANES - t, axis=1)
        _shift_cache(cache_ref, newc_ref, b, tail, t)

        q = jnp.concatenate([q_ref[0, tok], q_ref[1, tok]], axis=1)
        qb = jnp.where(hm, jnp.concatenate([q] * ATT_HPG, axis=0), 0.0).astype(BF16)
        kt = cache_ref[b, 0:kw, :].astype(BF16)
        vt = cache_ref[b, kw:2 * kw, :].astype(BF16)
        s_c = _dot(qb, kt) + tblc_ref[...]
        s_n = _dot_nt(qb, knp.astype(BF16)) + tbln_ref[...]
        m = jnp.maximum(jnp.max(s_c, axis=-1, keepdims=True), jnp.max(s_n, axis=-1, keepdims=True))
        e_c = jnp.exp(s_c - m)
        e_n = jnp.exp(s_n - m)
        den = jnp.sum(e_c, axis=-1, keepdims=True) + jnp.sum(e_n, axis=-1, keepdims=True)
        o = _dot_nt((e_c / den).astype(BF16), vt) + _dot((e_n / den).astype(BF16), vnp.astype(BF16))
        lse = m + jnp.log(den)
        o = jnp.where(hm, o, 0.0)
        lse = jnp.where(hm, lse, 0.0)
        o_sel = o[0:t]
        lse_sel = lse[0:t]
        for h in range(1, ATT_HPG):
            o_sel = o_sel + o[h * t:(h + 1) * t]
            lse_sel = lse_sel + lse[h * t:(h + 1) * t]
        for s in range(SLABS_PER_GROUP):
            o_ref[s, tok, :] = o_sel[:, s * LANES:(s + 1) * LANES]
            lse_ref[s, tok, :] = lse_sel[:, s * LANES:(s + 1) * LANES]


def _cache_view(cache):
    batch, wc = cache.shape[:2]
    return jnp.transpose(cache, (0, 2, 3, 4, 1)).reshape(batch, 2 * ATT_OUT_W, wc)


def _cache_unview(cache_t):
    batch, _, wc = cache_t.shape
    return jnp.transpose(cache_t.reshape(batch, 2, ATT_HPG, ATT_DH, wc), (0, 4, 1, 2, 3))


def _attn_sample(qa, ka, va, cache, group, batch, t):
    win, dil = ATT_GROUPS[group]
    n_back = win // dil
    wc = cache.shape[1]
    assert wc == win and wc % LANES == 0 and t % SUBLANES == 0 and t <= BAND
    kw = ATT_OUT_W
    bb = max(1, min(ATTN_SAMPLE_BLOCK_BYTES // (2 * kw * wc * 4), ATTN_SAMPLE_MAX_BB, batch))
    assert batch % bb == 0
    slopes = _alibi_slopes()[group * ATT_HPG:(group + 1) * ATT_HPG]
    qi = jnp.arange(t)[:, None]
    idx = jnp.concatenate([jnp.arange(wc), wc + jnp.arange(BAND)])[None, :]
    dist = wc + qi - idx
    valid = (dist >= 0) & (dist % dil == 0) & (dist <= dil * n_back) & (idx < wc + t)
    tbl = jnp.where(valid[None], -slopes[:, None, None] * dist.astype(F32)[None], NEG_INF)
    tbl = tbl.reshape(ATT_HPG * t, wc + BAND)
    tblc, tbln = tbl[:, :wc], tbl[:, wc:]

    s0 = group
    new_spec = pl.BlockSpec((SLABS_PER_GROUP, bb * t, LANES), lambda b: (s0, b, 0))
    out_spec = pl.BlockSpec((SLABS_PER_GROUP, bb * t, LANES), lambda b: (0, b, 0))
    cache_spec = pl.BlockSpec((bb, 2 * kw, wc), lambda b: (b, 0, 0))
    out = jax.ShapeDtypeStruct((SLABS_PER_GROUP, batch * t, LANES), F32)
    newc, o, lse = pl.pallas_call(
        functools.partial(_attn_sample_kernel, wc=wc, t=t, bb=bb),
        grid=(batch // bb,),
        in_specs=[
            cache_spec, new_spec, new_spec, new_spec,
            pl.BlockSpec(tblc.shape, lambda b: (0, 0)),
            pl.BlockSpec(tbln.shape, lambda b: (0, 0)),
        ],
        out_specs=[cache_spec, out_spec, out_spec],
        out_shape=[jax.ShapeDtypeStruct((batch, 2 * kw, wc), F32), out, out],
        compiler_params=_cparams(("parallel",)),
        name=f"attn_sample_g{group}",
    )(_cache_view(cache), qa, ka, va, tblc, tbln)
    return o, lse, _cache_unview(newc)


def _merge_kernel(x_ref, oret_ref, gate_ref, o0_ref, o1_ref, o2_ref, l0_ref, l1_ref, l2_ref,
                  wa_ref, wb_ref, wo_ref, x1_ref, *scratch, dilated, tm):
    def rows(ref, g, s, kind):
        dil = ATT_GROUPS[g][1]
        if not dilated or dil == 1:
            return ref[s]
        scr = scratch[0]
        slot = ((g - 1) * SLABS_PER_GROUP + s) * 2 + kind
        for r in range(dil):
            scr[slot, pl.ds(r, tm // dil, stride=dil), :] = ref[s, :, r * LANES:(r + 1) * LANES]
        return scr[slot]

    slabs = []
    for s in range(SLABS_PER_GROUP):
        ls = [rows(ref, g, s, 0) for g, ref in enumerate((l0_ref, l1_ref, l2_ref))]
        os_ = [rows(ref, g, s, 1) for g, ref in enumerate((o0_ref, o1_ref, o2_ref))]
        m = jnp.maximum(jnp.maximum(ls[0], ls[1]), ls[2])
        es = [jnp.exp(l - m) for l in ls]
        den = es[0] + es[1] + es[2]
        slabs.append((es[0] / den) * os_[0] + (es[1] / den) * os_[1] + (es[2] / den) * os_[2])
    o_att = jnp.concatenate(slabs, axis=1).astype(BF16)
    ga = gate_ref[:, 0:D_MODEL].astype(F32)
    gb = gate_ref[:, D_MODEL:GATE_W].astype(F32)
    merged = _sigmoid(ga) * _dot(oret_ref[...], wa_ref[...]) + _sigmoid(gb) * _dot(o_att, wb_ref[...])
    x1_ref[...] = x_ref[...] + _dot(merged.astype(BF16), wo_ref[...])


def _merge(x, oret, zgate, os_, ls, wa, wb, wo, dilated):
    n = x.shape[0]
    tm = min(TM_MERGE, n)
    row = lambda w: pl.BlockSpec((tm, w), lambda i: (i, 0))
    full = lambda a: pl.BlockSpec(a.shape, lambda i: (0, 0))

    def slab(g):
        dil = ATT_GROUPS[g][1] if dilated else 1
        return pl.BlockSpec((SLABS_PER_GROUP, tm // dil, dil * LANES), lambda i: (0, i, 0))

    slabs = [slab(g) for g in range(len(ATT_GROUPS))]
    n_scr = (len(ATT_GROUPS) - 1) * SLABS_PER_GROUP * 2
    return pl.pallas_call(
        functools.partial(_merge_kernel, dilated=dilated, tm=tm),
        grid=(n // tm,),
        in_specs=[row(D_MODEL), row(RET_V_W), row(GATE_W)] + slabs + slabs + [full(wa), full(wb), full(wo)],
        out_specs=row(D_MODEL),
        out_shape=jax.ShapeDtypeStruct((n, D_MODEL), F32),
        scratch_shapes=[pltpu.VMEM((n_scr, tm, LANES), F32)] if dilated else [],
        compiler_params=_cparams(("parallel",)),
        name="merge_dilated" if dilated else "merge",
    )(x, oret, zgate, *os_, *ls, wa, wb, wo)


def _router_kernel(xp_ref, xs_ref, g_ref, wr_ref, br_ref, tri_ref, h_ref, idx_ref, rank_ref, w_ref, cnt_ref,
                   run_scr, *, n_prompt_tiles, tm):
    i = pl.program_id(0)

    @pl.when(i == 0)
    def _():
        run_scr[...] = jnp.zeros_like(run_scr)

    x = jnp.where(i < n_prompt_tiles, xp_ref[...], xs_ref[...])
    h = _rms(x, g_ref[...])
    for j in range(ROW_TILES):
        h_ref[pl.ds(j, tm, stride=ROW_TILES), :] = h[:, j * LANES:(j + 1) * LANES]

    logits = jnp.dot(h, wr_ref[...], precision=lax.Precision.HIGHEST, preferred_element_type=F32) + br_ref[...]
    lanes = lax.broadcasted_iota(I32, (1, LANES), 1)
    lanes_f = lanes.astype(F32)
    work = logits
    vals, idxs, hots = [], [], []
    for _ in range(TOP_K):
        m = jnp.max(work, axis=-1, keepdims=True)
        idx = jnp.min(jnp.where(work == m, lanes_f, float(LANES)), axis=-1, keepdims=True)
        hot = lanes_f == idx
        vals.append(m)
        idxs.append(idx)
        hots.append(hot)
        work = jnp.where(hot, -jnp.inf, work)
    exps = [jnp.exp(v - vals[0]) for v in vals]
    den = exps[0] + exps[1] + exps[2] + exps[3]

    cat = jnp.concatenate([hot.astype(BF16) for hot in hots], axis=1)
    cum = _dot(tri_ref[...], cat)
    prev = run_scr[...]
    idx_out = jnp.zeros((tm, LANES), F32)
    rank_out = jnp.zeros((tm, LANES), F32)
    w_out = jnp.zeros((tm, LANES), F32)
    for k in range(TOP_K):
        hot_f = hots[k].astype(F32)
        rank = jnp.sum(hot_f * (cum[:, k * LANES:(k + 1) * LANES] + prev), axis=-1, keepdims=True)
        prev = prev + jnp.sum(hot_f, axis=0, keepdims=True)
        idx_out = jnp.where(lanes == k, idxs[k], idx_out)
        rank_out = jnp.where(lanes == k, rank, rank_out)
        w_out = jnp.where(lanes == k, exps[k] / den, w_out)
    run_scr[...] = prev
    idx_ref[...] = idx_out.T[:SUBLANES].astype(I32)
    rank_ref[...] = rank_out.T[:SUBLANES].astype(I32)
    w_ref[...] = w_out
    cnt_ref[...] = prev.astype(I32)


def _router(x1p, x1s, g_moe, wr_pad, br_pad):
    tm = TM_ROUTE
    npt, nst = x1p.shape[0] // tm, x1s.shape[0] // tm
    n = x1p.shape[0] + x1s.shape[0]
    tri = (jnp.arange(tm)[:, None] > jnp.arange(tm)[None, :]).astype(BF16)
    row128 = pl.BlockSpec((tm, LANES), lambda i: (i, 0))
    choice = pl.BlockSpec((SUBLANES, tm), lambda i: (0, i))
    return pl.pallas_call(
        functools.partial(_router_kernel, n_prompt_tiles=npt, tm=tm),
        grid=(npt + nst,),
        in_specs=[
            pl.BlockSpec((tm, D_MODEL), lambda i: (jnp.minimum(i, npt - 1), 0)),
            pl.BlockSpec((tm, D_MODEL), lambda i: (jnp.clip(i - npt, 0, nst - 1), 0)),
            pl.BlockSpec((1, D_MODEL), lambda i: (0, 0)),
            pl.BlockSpec((D_MODEL, LANES), lambda i: (0, 0)),
            pl.BlockSpec((1, LANES), lambda i: (0, 0)),
            pl.BlockSpec((tm, tm), lambda i: (0, 0)),
        ],
        out_specs=[
            pl.BlockSpec((tm * ROW_TILES, LANES), lambda i: (i, 0)),
            choice, choice, row128,
            pl.BlockSpec((1, LANES), lambda i: (0, 0)),
        ],
        out_shape=[
            jax.ShapeDtypeStruct((n * ROW_TILES, LANES), F32),
            jax.ShapeDtypeStruct((SUBLANES, n), I32),
            jax.ShapeDtypeStruct((SUBLANES, n), I32),
            jax.ShapeDtypeStruct((n, LANES), F32),
            jax.ShapeDtypeStruct((1, LANES), I32),
        ],
        scratch_shapes=[pltpu.VMEM((1, LANES), F32)],
        compiler_params=_cparams(("arbitrary",)),
        name="router",
    )(x1p, x1s, g_moe, wr_pad, br_pad, tri)


def _row_copy(src_ref, src_row, dst_ref, dst_row, sem):
    return pltpu.make_async_copy(
        src_ref.at[pl.ds(pl.multiple_of(src_row * ROW_TILES, ROW_TILES), ROW_TILES), :],
        dst_ref.at[pl.ds(pl.multiple_of(dst_row * ROW_TILES, ROW_TILES), ROW_TILES), :],
        sem)


def _dispatch_kernel(zstart_ref, zlen_ref, nused_ref, slots_ref, h_ref, hs_ref, zero_scr, sem, *, tm, n_tiles):
    i = pl.program_id(0)

    def zero_copy(row, n_rows):
        start = pl.multiple_of(row * ROW_TILES, ROW_TILES)
        return pltpu.make_async_copy(zero_scr.at[pl.ds(0, n_rows * ROW_TILES), :],
                                     hs_ref.at[pl.ds(start, n_rows * ROW_TILES), :], sem)

    def pad_copies(e, op):
        row, left = zstart_ref[e], zlen_ref[e]
        for b in reversed(range(TM_MOE.bit_length() - 1)):
            bit = (left >> b) & 1

            @pl.when(bit == 1)
            def _():
                op(zero_copy(row, 1 << b))

            row = row + (bit << b)

    def all_copies(op):
        lax.fori_loop(0, N_EXPERTS, lambda e, c: (pad_copies(e, op), c)[1], 0)
        lax.fori_loop(nused_ref[0], n_tiles, lambda tt, c: (op(zero_copy(tt * TM_MOE, TM_MOE)), c)[1], 0)

    @pl.when(i == 0)
    def _():
        zero_scr[...] = jnp.zeros_like(zero_scr)
        all_copies(lambda c: c.start())
        all_copies(lambda c: c.wait())

    def body(tok, carry):
        for k in range(TOP_K):
            _row_copy(h_ref, tok, hs_ref, slots_ref[0, 0, k * tm + tok], sem).start(priority=k % 2)
        return carry

    lax.fori_loop(0, tm, body, 0, unroll=8)
    for _ in range(TOP_K):
        pltpu.make_async_copy(h_ref, hs_ref.at[pl.ds(0, tm * ROW_TILES), :], sem).wait()


def _dispatch(hrow, slots, zstart, zlen, n_used, n_tiles):
    tm = TM_ROUTE
    n = hrow.shape[0] // ROW_TILES
    nt = n // tm
    slots3 = jnp.transpose(slots.reshape(TOP_K, nt, tm), (1, 0, 2)).reshape(nt, 1, TOP_K * tm)
    grid_spec = pltpu.PrefetchScalarGridSpec(
        num_scalar_prefetch=3,
        grid=(nt,),
        in_specs=[
            pl.BlockSpec((1, 1, tm * TOP_K), lambda i, z, l, u: (i, 0, 0), memory_space=pltpu.SMEM),
            pl.BlockSpec((tm * ROW_TILES, LANES), lambda i, z, l, u: (i, 0)),
        ],
        out_specs=pl.BlockSpec(memory_space=pl.ANY),
        scratch_shapes=[pltpu.VMEM((TM_MOE * ROW_TILES, LANES), F32), pltpu.SemaphoreType.DMA(())],
    )
    return pl.pallas_call(
        functools.partial(_dispatch_kernel, tm=tm, n_tiles=n_tiles),
        grid_spec=grid_spec,
        out_shape=jax.ShapeDtypeStruct((n_tiles * TM_MOE * ROW_TILES, LANES), F32),
        compiler_params=_cparams(("arbitrary",)),
        name="dispatch",
    )(zstart, zlen, n_used, slots3, hrow)


def _experts_kernel(texp_ref, nused_ref, hs_ref, wu_ref, bu_ref, wd_ref, bd_ref, ys_ref, wu_scr, wd_scr):
    j = pl.program_id(0)
    tm = TM_MOE
    n_used = nused_ref[0]
    jj = jnp.minimum(j, n_used - 1)
    new_expert = (j == 0) | (texp_ref[jj] != texp_ref[jnp.maximum(jj - 1, 0)])

    @pl.when((j < n_used) & new_expert)
    def _():
        wu_scr[...] = wu_ref[0].astype(BF16)
        wd_scr[...] = wd_ref[0].astype(BF16)

    @pl.when(j < n_used)
    def _():
        x = jnp.concatenate(
            [hs_ref[pl.ds(c, tm, stride=ROW_TILES), :] for c in range(ROW_TILES)], axis=1).astype(BF16)
        u = _dot(x, wu_scr[...]) + bu_ref[0]
        glu = jnp.minimum(u[:, :D_EXPERT], SWIGLU_LIMIT)
        lin = jnp.clip(u[:, D_EXPERT:], -SWIGLU_LIMIT, SWIGLU_LIMIT)
        a = glu * _sigmoid(SWIGLU_ALPHA * glu) * (lin + 1.0)
        y = _dot(a.astype(BF16), wd_scr[...]) + bd_ref[0]
        for c in range(ROW_TILES):
            ys_ref[pl.ds(c, tm, stride=ROW_TILES), :] = y[:, c * LANES:(c + 1) * LANES]

    @pl.when(j >= n_used)
    def _():
        ys_ref[...] = jnp.zeros_like(ys_ref)


def _experts(hs, tile_expert, n_used, w_up, b_up, w_down, b_down, n_tiles):
    tm = TM_MOE
    tile = lambda j, te, nu: (jnp.minimum(j, nu[0] - 1), 0)
    exp3 = lambda j, te, nu: (te[jnp.minimum(j, nu[0] - 1)], 0, 0)
    grid_spec = pltpu.PrefetchScalarGridSpec(
        num_scalar_prefetch=2,
        grid=(n_tiles,),
        in_specs=[
            pl.BlockSpec((tm * ROW_TILES, LANES), tile),
            pl.BlockSpec((1, D_MODEL, 2 * D_EXPERT), exp3),
            pl.BlockSpec((1, 1, 2 * D_EXPERT), exp3),
            pl.BlockSpec((1, D_EXPERT, D_MODEL), exp3),
            pl.BlockSpec((1, 1, D_MODEL), exp3),
        ],
        out_specs=pl.BlockSpec((tm * ROW_TILES, LANES), lambda j, te, nu: (j, 0)),
        scratch_shapes=[pltpu.VMEM((D_MODEL, 2 * D_EXPERT), BF16), pltpu.VMEM((D_EXPERT, D_MODEL), BF16)],
    )
    return pl.pallas_call(
        _experts_kernel,
        grid_spec=grid_spec,
        out_shape=jax.ShapeDtypeStruct((n_tiles * tm * ROW_TILES, LANES), F32),
        compiler_params=_cparams(("arbitrary",)),
        name="experts",
    )(tile_expert, n_used, hs, w_up, b_up.reshape(N_EXPERTS, 1, -1), w_down, b_down.reshape(N_EXPERTS, 1, -1))


def _combine_kernel(slots_ref, nslots_ref, x1_ref, w_ref, p_ref, g_ref, wpg_ref, wpe_ref, ys_ref, y_ref,
                    buf, x2_scr, sems, *, tm):
    i = pl.program_id(0)
    cur = lax.rem(i, 2)
    nxt = 1 - cur

    def gather(sref, half, tok, k):
        return _row_copy(ys_ref, sref[0, 0, k * tm + tok], buf.at[half, k], tok, sems.at[half])

    def wait_tile(half):
        for k in range(TOP_K):
            pltpu.make_async_copy(ys_ref.at[pl.ds(0, tm * ROW_TILES), :], buf.at[half, k], sems.at[half]).wait()

    @pl.when(i == 0)
    def _():
        def body(tok, carry):
            for k in range(TOP_K):
                gather(slots_ref, 0, tok, k).start(priority=k % 2)
            return carry

        lax.fori_loop(0, tm, body, 0)

    wait_tile(cur)
    rg = COMB_ROW_GROUP
    per = tm // (tm // rg * ROW_TILES)
    for r in range(tm // rg):
        w = w_ref[r * rg:(r + 1) * rg, :]
        wk = [jnp.broadcast_to(w[:, k:k + 1], (rg, LANES)) for k in range(TOP_K)]
        for c in range(ROW_TILES):
            acc = x1_ref[r * rg:(r + 1) * rg, c * LANES:(c + 1) * LANES]
            for k in range(TOP_K):
                acc = acc + wk[k] * buf[cur, k, pl.ds(r * rg * ROW_TILES + c, rg, stride=ROW_TILES), :]
            g = r * ROW_TILES + c
            for tok in range(g * per, (g + 1) * per):
                for k in range(TOP_K):
                    gather(nslots_ref, nxt, tok, k).start(priority=k % 2)
            x2_scr[r * rg:(r + 1) * rg, c * LANES:(c + 1) * LANES] = acc
    x2 = x2_scr[...]
    gate = _sigmoid(_dot(_rms(x2, g_ref[...]).astype(BF16), wpg_ref[...]))
    y_ref[...] = x2 + gate * _dot(p_ref[...].astype(BF16), wpe_ref[...])

    @pl.when(i == pl.num_programs(0) - 1)
    def _():
        wait_tile(nxt)


def _combine(x1, slots, topw, p, g_ple, wpg, wpe, ys):
    n = x1.shape[0]
    tm = TM_COMB
    nt = n // tm
    slots3 = jnp.transpose(slots.reshape(TOP_K, nt, tm), (1, 0, 2)).reshape(nt, 1, TOP_K * tm)
    full = lambda a: pl.BlockSpec(a.shape, lambda i: (0, 0))
    return pl.pallas_call(
        functools.partial(_combine_kernel, tm=tm),
        grid=(nt,),
        in_specs=[
            pl.BlockSpec((1, 1, tm * TOP_K), lambda i: (i, 0, 0), memory_space=pltpu.SMEM),
            pl.BlockSpec((1, 1, tm * TOP_K), lambda i: (jnp.minimum(i + 1, nt - 1), 0, 0), memory_space=pltpu.SMEM),
            pl.BlockSpec((tm, D_MODEL), lambda i: (i, 0)),
            pl.BlockSpec((tm, LANES), lambda i: (i, 0)),
            pl.BlockSpec((tm, PLE_DIM), lambda i: (i, 0)),
            full(g_ple), full(wpg), full(wpe),
            pl.BlockSpec(memory_space=pl.ANY),
        ],
        out_specs=pl.BlockSpec((tm, D_MODEL), lambda i: (i, 0)),
        out_shape=jax.ShapeDtypeStruct((n, D_MODEL), F32),
        scratch_shapes=[pltpu.VMEM((2, TOP_K, tm * ROW_TILES, LANES), F32), pltpu.VMEM((tm, D_MODEL), F32),
                        pltpu.SemaphoreType.DMA((2,))],
        compiler_params=_cparams(("arbitrary",)),
        name="combine",
    )(slots3, slots3, x1, topw, p, g_ple, wpg, wpe, ys)


def _prompt_cache_kernel(k_ref, v_ref, o_ref, *scratch, dil, keep):
    for kv, ref in enumerate((k_ref, v_ref)):
        for s in range(SLABS_PER_GROUP):
            if dil == 1:
                tok = ref[s]
            else:
                scr = scratch[0]
                for r in range(dil):
                    scr[pl.ds(r, keep // dil, stride=dil), :] = ref[s, :, r * LANES:(r + 1) * LANES]
                tok = scr[...]
            row0 = kv * ATT_OUT_W + s * LANES
            o_ref[0, row0:row0 + LANES, :] = tok.T


def _cache_from_slabs(ka, va, group, batch, seq, keep):
    dil = ATT_GROUPS[group][1]
    assert keep % dil == 0 and seq % keep == 0 and keep % LANES == 0
    blocks_per_seq = seq // keep
    tail = pl.BlockSpec((SLABS_PER_GROUP, keep // dil, dil * LANES), lambda b: (0, (b + 1) * blocks_per_seq - 1, 0))
    cache_t = pl.pallas_call(
        functools.partial(_prompt_cache_kernel, dil=dil, keep=keep),
        grid=(batch,),
        in_specs=[tail, tail],
        out_specs=pl.BlockSpec((1, 2 * ATT_OUT_W, keep), lambda b: (b, 0, 0)),
        out_shape=jax.ShapeDtypeStruct((batch, 2 * ATT_OUT_W, keep), F32),
        scratch_shapes=[pltpu.VMEM((keep, LANES), F32)] if dil > 1 else [],
        compiler_params=_cparams(("parallel",)),
        name=f"prompt_cache_g{group}",
    )(ka, va)
    return _cache_unview(cache_t)


def _layer(xp, xs, pp, ps, state, caches, g_mix, w_in, ret_gn, qn_g, kn_g, w_a, w_b, w_o, g_moe, w_router, b_router,
           w_up, b_up, w_down, b_down, g_ple, w_ple_gate, w_ple):
    batch, seq, _ = xp.shape
    dbatch, dseq, _ = xs.shape
    n_p, n_s = batch * seq, dbatch * dseq
    n = n_p + n_s

    w_in_bf = w_in.astype(BF16)
    g_mix2 = g_mix.reshape(1, D_MODEL)
    qg = qn_g.reshape(N_SLABS, 1, LANES)
    kg = kn_g.reshape(N_SLABS, 1, LANES)
    gn = ret_gn.reshape(RET_HEADS, 1, RET_DV)
    wa, wb, wo = w_a.astype(BF16), w_b.astype(BF16), w_o.astype(BF16)

    xp2 = xp.reshape(n_p, D_MODEL)
    xs2 = xs.reshape(n_s, D_MODEL)
    zret_p, zgate_p, *qkv_p = _inproj(xp2, g_mix2, w_in_bf, qg, kg, dilated=True)
    zret_s, zgate_s, qa_s, ka_s, va_s = _inproj(xs2, g_mix2, w_in_bf, qg, kg, dilated=False)

    oret_p, state_p = _ret_prompt(zret_p, gn, batch, seq)
    oret_s, state_s = _ret_sample(zret_s, state, gn, dbatch, dseq)

    o_p, l_p, o_s, l_s, win_p, win_s = [], [], [], [], [], []
    for g, (win, _) in enumerate(ATT_GROUPS):
        qa_g, ka_g, va_g = qkv_p[3 * g:3 * g + 3]
        o, l = _attn_prompt(qa_g, ka_g, va_g, g, batch, seq)
        o_p.append(o)
        l_p.append(l)
        win_p.append(_cache_from_slabs(ka_g, va_g, g, batch, seq, min(win, seq)))
        o, l, newc = _attn_sample(qa_s, ka_s, va_s, caches[g], g, dbatch, dseq)
        o_s.append(o)
        l_s.append(l)
        win_s.append(newc)

    x1p = _merge(xp2, oret_p, zgate_p, o_p, l_p, wa, wb, wo, dilated=True)
    x1s = _merge(xs2, oret_s, zgate_s, o_s, l_s, wa, wb, wo, dilated=False)

    wr_pad = jnp.zeros((D_MODEL, LANES), F32).at[:, :N_EXPERTS].set(w_router)
    br_pad = jnp.full((1, LANES), NEG_INF, F32).at[0, :N_EXPERTS].set(b_router)
    hrow, top_i, rank, top_w, counts = _router(x1p, x1s, g_moe.reshape(1, D_MODEL), wr_pad, br_pad)

    cnt = counts[0, :N_EXPERTS]
    padded = (cnt + TM_MOE - 1) // TM_MOE * TM_MOE
    ends = jnp.cumsum(padded)
    goff = ends - padded
    slots = rank[:TOP_K]
    for e in range(N_EXPERTS):
        slots = slots + jnp.where(top_i[:TOP_K] == e, goff[e], 0)
    n_tiles = (n * TOP_K) // TM_MOE + N_EXPERTS
    tile_ends = ends // TM_MOE
    n_used = tile_ends[-1:].astype(I32)
    tile_expert = jnp.sum(tile_ends[None, :] <= jnp.arange(n_tiles)[:, None], axis=1)
    tile_expert = jnp.minimum(tile_expert, N_EXPERTS - 1).astype(I32)
    zstart = (goff + cnt).astype(I32)
    zlen = (padded - cnt).astype(I32)

    hs = _dispatch(hrow, slots, zstart, zlen, n_used, n_tiles)
    ys = _experts(hs, tile_expert, n_used, w_up, b_up, w_down, b_down, n_tiles)

    g_ple2 = g_ple.reshape(1, D_MODEL)
    wpg, wpe = w_ple_gate.astype(BF16), w_ple.astype(BF16)
    yp = _combine(x1p, slots[:, :n_p], top_w[:n_p], pp.reshape(n_p, PLE_DIM), g_ple2, wpg, wpe, ys)
    ys_out = _combine(x1s, slots[:, n_p:], top_w[n_p:], ps.reshape(n_s, PLE_DIM), g_ple2, wpg, wpe, ys)
    return (yp.reshape(batch, seq, D_MODEL), ys_out.reshape(dbatch, dseq, D_MODEL), state_p, state_s, win_p, win_s)


def kernel(x_prompt, x_sample, p_prompt, p_sample, state_ret, cache_win128_kv, cache_win512_kv, cache_win2048_kv, norm_mix_g, w_in, ret_norm_g, q_norm_g, k_norm_g, w_a, w_b, w_o, norm_moe_g, w_router, b_router, w_up, b_up, w_down, b_down, norm_ple_g, w_ple_gate, w_ple):
    caches = (cache_win128_kv, cache_win512_kv, cache_win2048_kv)
    depth = w_in.shape[0]
    xp, xs = x_prompt, x_sample
    ret_p, ret_s = [], []
    win_p = [[] for _ in ATT_GROUPS]
    win_s = [[] for _ in ATT_GROUPS]
    for i in range(depth):
        xp, xs, sp, ss, bp, bs = _layer(
            xp, xs, p_prompt[i], p_sample[i], state_ret[i], [c[i] for c in caches], norm_mix_g[i], w_in[i],
            ret_norm_g[i], q_norm_g[i], k_norm_g[i], w_a[i], w_b[i], w_o[i], norm_moe_g[i], w_router[i], b_router[i],
            w_up[i], b_up[i], w_down[i], b_down[i], norm_ple_g[i], w_ple_gate[i], w_ple[i])
        ret_p.append(sp)
        ret_s.append(ss)
        for g in range(len(ATT_GROUPS)):
            win_p[g].append(bp[g])
            win_s[g].append(bs[g])
    return (xp, xs, jnp.stack(ret_p), jnp.stack(ret_s), jnp.stack(win_p[0]), jnp.stack(win_s[0]),
            jnp.stack(win_p[1]), jnp.stack(win_s[1]), jnp.stack(win_p[2]), jnp.stack(win_s[2]))
```

```python
import functools

import jax
import jax.numpy as jnp
from jax import lax
from jax.experimental import pallas as pl
from jax.experimental.pallas import tpu as pltpu

F32 = jnp.float32
BF16 = jnp.bfloat16
I32 = jnp.int32

D_MODEL = 1024
RET_HEADS = 4
RET_DK = 128
RET_DV = 256
RET_QK_W = RET_HEADS * RET_DK
RET_V_W = RET_HEADS * RET_DV
RET_CHUNK = 128
ATT_GROUPS = ((128, 1), (512, 4), (2048, 16))
ATT_HPG = 4
ATT_HEADS = ATT_HPG * len(ATT_GROUPS)
ATT_DH = 64
ATT_W = ATT_HEADS * ATT_DH
ATT_OUT_W = ATT_HPG * ATT_DH
BAND = 128
N_SLABS = ATT_W // 128
SLABS_PER_GROUP = ATT_OUT_W // 128
N_EXPERTS = 32
TOP_K = 4
D_EXPERT = 1024
SWIGLU_LIMIT = 7.0
SWIGLU_ALPHA = 1.702
PLE_DIM = 256
EPS = 1e-6
NEG_INF = -1e30

OFF_QR = 0
OFF_KR = OFF_QR + RET_QK_W
OFF_VR = OFF_KR + RET_QK_W
OFF_GR = OFF_VR + RET_V_W
OFF_QA = OFF_GR + RET_V_W
OFF_KA = OFF_QA + ATT_W
OFF_VA = OFF_KA + ATT_W
OFF_GA = OFF_VA + ATT_W
OFF_GB = OFF_GA + D_MODEL
N_IN = OFF_GB + D_MODEL
RET_W = OFF_QA
GATE_W = 2 * D_MODEL

LANES = 128
SUBLANES = 8
ROW_TILES = D_MODEL // LANES
VMEM_LIMIT = 56 * 1024 * 1024

TM_INPROJ = 512
TM_MERGE = 512
TM_ROUTE = 512
TM_MOE = 512
TM_COMB = 512
COMB_ROW_GROUP = 16
RET_CHUNKS_PER_STEP = 8
RET_SAMPLE_BB = 8
ATTN_Q_BLOCKS = 16
ATTN_SAMPLE_BLOCK_BYTES = 8 * 1024 * 1024
ATTN_SAMPLE_MAX_BB = 16


def _cparams(sem):
    return pltpu.CompilerParams(dimension_semantics=sem, vmem_limit_bytes=VMEM_LIMIT)


def _dot(a, b):
    return jnp.dot(a, b, preferred_element_type=F32)


def _dot_nt(a, b):
    return lax.dot_general(a, b, (((1,), (1,)), ((), ())), preferred_element_type=F32)


def _dot_tn(a, b):
    return lax.dot_general(a, b, (((0,), (0,)), ((), ())), preferred_element_type=F32)


def _rms(x, g):
    return x * lax.rsqrt(jnp.mean(x * x, axis=-1, keepdims=True) + EPS) * g


def _sigmoid(x):
    return 1.0 / (1.0 + jnp.exp(-x))


def _dilated_cols(dil):
    return dil * LANES


def _inproj_kernel(x_ref, g_ref, w_ref, qg_ref, kg_ref, zret_ref, zgate_ref, *rest, dilated, tm):
    h = _rms(x_ref[...], g_ref[...]).astype(BF16)

    def mm(lo, hi):
        return _dot(h, w_ref[:, lo:hi])

    zret_ref[:, OFF_QR:OFF_KR] = mm(OFF_QR, OFF_KR).astype(BF16)
    zret_ref[:, OFF_KR:OFF_VR] = (mm(OFF_KR, OFF_VR) * (RET_DK ** -0.5)).astype(BF16)
    zret_ref[:, OFF_VR:OFF_QA] = mm(OFF_VR, OFF_QA).astype(BF16)
    zgate_ref[...] = mm(OFF_GA, N_IN).astype(BF16)

    lane_lo = lax.broadcasted_iota(I32, (1, LANES), 1) < ATT_DH

    def headnorm(x, g):
        x2 = x * x
        lo = jnp.sum(jnp.where(lane_lo, x2, 0.0), axis=-1, keepdims=True)
        hi = jnp.sum(jnp.where(lane_lo, 0.0, x2), axis=-1, keepdims=True)
        ms = jnp.where(lane_lo, lo, hi) * (1.0 / ATT_DH)
        return x * lax.rsqrt(ms + EPS) * g

    q = mm(OFF_QA, OFF_KA)
    k = mm(OFF_KA, OFF_VA)
    v = mm(OFF_VA, OFF_GA)
    for s in range(N_SLABS):
        sl = slice(s * LANES, (s + 1) * LANES)
        vals = (headnorm(q[:, sl], qg_ref[s]) * (ATT_DH ** -0.5), headnorm(k[:, sl], kg_ref[s]), v[:, sl])
        g, sg = divmod(s, SLABS_PER_GROUP)
        dil = ATT_GROUPS[g][1]
        for a, val in enumerate(vals):
            if not dilated:
                rest[a][s] = val
            elif dil == 1:
                rest[3 * g + a][sg] = val
            else:
                scr = rest[-1]
                slot = (s - SLABS_PER_GROUP) * 3 + a
                scr[slot] = val
                for r in range(dil):
                    rest[3 * g + a][sg, :, r * LANES:(r + 1) * LANES] = scr[slot, pl.ds(r, tm // dil, stride=dil), :]


def _inproj(x, g_mix, w_in_bf, qg, kg, dilated):
    n = x.shape[0]
    tm = TM_INPROJ
    if dilated:
        slab_shapes, slab_specs = [], []
        for _, dil in ATT_GROUPS:
            assert tm % (dil * SUBLANES) == 0
            slab_shapes += [jax.ShapeDtypeStruct((SLABS_PER_GROUP, n // dil, _dilated_cols(dil)), F32)] * 3
            slab_specs += [pl.BlockSpec((SLABS_PER_GROUP, tm // dil, _dilated_cols(dil)), lambda i: (0, i, 0))] * 3
        scratch = [pltpu.VMEM(((N_SLABS - SLABS_PER_GROUP) * 3, tm, LANES), F32)]
    else:
        slab_shapes = [jax.ShapeDtypeStruct((N_SLABS, n, LANES), F32)] * 3
        slab_specs = [pl.BlockSpec((N_SLABS, tm, LANES), lambda i: (0, i, 0))] * 3
        scratch = []
    return pl.pallas_call(
        functools.partial(_inproj_kernel, dilated=dilated, tm=tm),
        grid=(n // tm,),
        in_specs=[
            pl.BlockSpec((tm, D_MODEL), lambda i: (i, 0)),
            pl.BlockSpec((1, D_MODEL), lambda i: (0, 0)),
            pl.BlockSpec((D_MODEL, N_IN), lambda i: (0, 0), pipeline_mode=pl.Buffered(1)),
            pl.BlockSpec((N_SLABS, 1, LANES), lambda i: (0, 0, 0)),
            pl.BlockSpec((N_SLABS, 1, LANES), lambda i: (0, 0, 0)),
        ],
        out_specs=[pl.BlockSpec((tm, RET_W), lambda i: (i, 0)), pl.BlockSpec((tm, GATE_W), lambda i: (i, 0))]
        + slab_specs,
        out_shape=[jax.ShapeDtypeStruct((n, RET_W), BF16), jax.ShapeDtypeStruct((n, GATE_W), BF16)] + slab_shapes,
        scratch_shapes=scratch,
        compiler_params=_cparams(("parallel",)),
        name="inproj_dilated" if dilated else "inproj",
    )(x, g_mix, w_in_bf, qg, kg)


def _ret_tables(chunk):
    lg = jnp.log1p(-jnp.exp2(-5.0 - jnp.arange(RET_HEADS, dtype=F32)))
    pos = jnp.arange(chunk, dtype=F32)
    rel = pos[:, None] - pos[None, :]
    intra = jnp.where(rel[None] >= 0, jnp.exp(lg[:, None, None] * jnp.maximum(rel, 0.0)[None]), 0.0)
    q_dec = jnp.exp(lg[:, None] * (pos[None, :] + 1.0))
    k_dec = jnp.exp(lg[:, None] * (chunk - 1.0 - pos[None, :]))
    c_dec = jnp.exp(lg * chunk)
    q_dec = jnp.broadcast_to(q_dec[:, :, None], (RET_HEADS, chunk, RET_DV))
    k_dec = jnp.broadcast_to(k_dec[:, :, None], (RET_HEADS, chunk, RET_DK))
    c_dec = jnp.broadcast_to(c_dec[:, None, None], (RET_HEADS, 1, RET_DV))
    return intra, q_dec, k_dec, c_dec


def _ret_head(q, kf, v, gr, s, intra, q_dec, k_dec, c_dec, gn):
    qb = q.astype(BF16)
    vb = v.astype(BF16)
    gr = gr.astype(F32)
    sc = _dot_nt(qb, kf.astype(BF16)) * intra
    o = _dot(sc.astype(BF16), vb) + _dot(qb, s.astype(BF16)) * q_dec
    s_new = s * c_dec + _dot_tn((kf.astype(F32) * k_dec).astype(BF16), vb)
    y = _rms(o, gn) * (gr * _sigmoid(gr))
    return y, s_new


def _ret_prompt_kernel(q_ref, k_ref, v_ref, g_ref, intra_ref, qdec_ref, kdec_ref, cdec_ref, gn_ref,
                       o_ref, sfin_ref, s_scr):
    c_idx = pl.program_id(1)

    @pl.when(c_idx == 0)
    def _():
        s_scr[...] = jnp.zeros_like(s_scr)

    for c in range(RET_CHUNKS_PER_STEP):
        rows = slice(c * RET_CHUNK, (c + 1) * RET_CHUNK)
        for h in range(RET_HEADS):
            ks = slice(h * RET_DK, (h + 1) * RET_DK)
            vs = slice(h * RET_DV, (h + 1) * RET_DV)
            y, s_new = _ret_head(q_ref[rows, ks], k_ref[rows, ks], v_ref[rows, vs], g_ref[rows, vs], s_scr[h],
                                 intra_ref[h], qdec_ref[h], kdec_ref[h], cdec_ref[h], gn_ref[h])
            s_scr[h] = s_new
            o_ref[rows, vs] = y.astype(BF16)

    @pl.when(c_idx == pl.num_programs(1) - 1)
    def _():
        sfin_ref[0] = s_scr[...]


def _ret_prompt(zret, ret_gn, batch, seq):
    tc = RET_CHUNK * RET_CHUNKS_PER_STEP
    nc = seq // tc
    intra, q_dec, k_dec, c_dec = _ret_tables(RET_CHUNK)
    row = lambda b, c: b * nc + c
    const3 = lambda b, c: (0, 0, 0)
    return pl.pallas_call(
        _ret_prompt_kernel,
        grid=(batch, nc),
        in_specs=[
            pl.BlockSpec((tc, RET_QK_W), lambda b, c: (row(b, c), OFF_QR // RET_QK_W)),
            pl.BlockSpec((tc, RET_QK_W), lambda b, c: (row(b, c), OFF_KR // RET_QK_W)),
            pl.BlockSpec((tc, RET_V_W), lambda b, c: (row(b, c), OFF_VR // RET_V_W)),
            pl.BlockSpec((tc, RET_V_W), lambda b, c: (row(b, c), OFF_GR // RET_V_W)),
            pl.BlockSpec(intra.shape, const3),
            pl.BlockSpec(q_dec.shape, const3),
            pl.BlockSpec(k_dec.shape, const3),
            pl.BlockSpec(c_dec.shape, const3),
            pl.BlockSpec((RET_HEADS, 1, RET_DV), const3),
        ],
        out_specs=[
            pl.BlockSpec((tc, RET_V_W), lambda b, c: (row(b, c), 0)),
            pl.BlockSpec((1, RET_HEADS, RET_DK, RET_DV), lambda b, c: (b, 0, 0, 0)),
        ],
        out_shape=[
            jax.ShapeDtypeStruct((batch * seq, RET_V_W), BF16),
            jax.ShapeDtypeStruct((batch, RET_HEADS, RET_DK, RET_DV), F32),
        ],
        scratch_shapes=[pltpu.VMEM((RET_HEADS, RET_DK, RET_DV), F32)],
        compiler_params=_cparams(("parallel", "arbitrary")),
        name="ret_prompt",
    )(zret, zret, zret, zret, intra, q_dec, k_dec, c_dec, ret_gn)


def _ret_sample_kernel(q_ref, k_ref, v_ref, g_ref, s_ref, intra_ref, qdec_ref, kdec_ref, cdec_ref, gn_ref,
                       o_ref, snew_ref, *, t):
    for b in range(RET_SAMPLE_BB):
        rows = slice(b * t, (b + 1) * t)
        for h in range(RET_HEADS):
            ks = slice(h * RET_DK, (h + 1) * RET_DK)
            vs = slice(h * RET_DV, (h + 1) * RET_DV)
            y, s_new = _ret_head(q_ref[rows, ks], k_ref[rows, ks], v_ref[rows, vs], g_ref[rows, vs], s_ref[b, h],
                                 intra_ref[h], qdec_ref[h], kdec_ref[h], cdec_ref[h], gn_ref[h])
            snew_ref[b, h] = s_new
            o_ref[rows, vs] = y.astype(BF16)


def _ret_sample(zret, state, ret_gn, batch, t):
    bb = RET_SAMPLE_BB
    tr = bb * t
    intra, q_dec, k_dec, c_dec = _ret_tables(t)
    const3 = lambda i: (0, 0, 0)
    st_spec = pl.BlockSpec((bb, RET_HEADS, RET_DK, RET_DV), lambda i: (i, 0, 0, 0))
    return pl.pallas_call(
        functools.partial(_ret_sample_kernel, t=t),
        grid=(batch // bb,),
        in_specs=[
            pl.BlockSpec((tr, RET_QK_W), lambda i: (i, OFF_QR // RET_QK_W)),
            pl.BlockSpec((tr, RET_QK_W), lambda i: (i, OFF_KR // RET_QK_W)),
            pl.BlockSpec((tr, RET_V_W), lambda i: (i, OFF_VR // RET_V_W)),
            pl.BlockSpec((tr, RET_V_W), lambda i: (i, OFF_GR // RET_V_W)),
            st_spec,
            pl.BlockSpec(intra.shape, const3),
            pl.BlockSpec(q_dec.shape, const3),
            pl.BlockSpec(k_dec.shape, const3),
            pl.BlockSpec(c_dec.shape, const3),
            pl.BlockSpec((RET_HEADS, 1, RET_DV), const3),
        ],
        out_specs=[pl.BlockSpec((tr, RET_V_W), lambda i: (i, 0)), st_spec],
        out_shape=[
            jax.ShapeDtypeStruct((batch * t, RET_V_W), BF16),
            jax.ShapeDtypeStruct((batch, RET_HEADS, RET_DK, RET_DV), F32),
        ],
        compiler_params=_cparams(("parallel",)),
        name="ret_sample",
    )(zret, zret, zret, zret, state, intra, q_dec, k_dec, c_dec, ret_gn)


def _alibi_slopes():
    return jnp.exp2(-8.0 * (jnp.arange(ATT_HEADS, dtype=F32) + 1.0) / ATT_HEADS)


def _softmax_parts(s):
    m = jnp.max(s, axis=-1, keepdims=True)
    e = jnp.exp(s - m)
    den = jnp.sum(e, axis=-1, keepdims=True)
    return e / den, m + jnp.log(den)


def _attn_prompt_kernel(q_ref, kp_ref, kc_ref, vp_ref, vc_ref, tbl_ref, o_ref, lse_ref, *, qb, nres):
    i = pl.program_id(3)
    lane_lo = lax.broadcasted_iota(I32, (1, LANES), 1) < ATT_DH
    kcol = lax.broadcasted_iota(I32, (1, 2 * BAND), 1)
    first = jnp.where((i == 0) & (kcol < BAND), NEG_INF, 0.0)
    for rr in range(nres):
        lanes = slice(rr * LANES, (rr + 1) * LANES)
        k_all = jnp.concatenate([kp_ref[0, :, lanes], kc_ref[0, :, lanes]], axis=0).astype(BF16)
        v_all = jnp.concatenate([vp_ref[0, :, lanes], vc_ref[0, :, lanes]], axis=0).astype(BF16)
        for j in range(qb):
            rows = slice(j * BAND, (j + 1) * BAND)
            q = q_ref[0, rows, lanes]
            k = k_all[j * BAND:(j + 2) * BAND]
            v = v_all[j * BAND:(j + 2) * BAND]
            outs, lses = [], []
            for hh in range(2):
                keep = lane_lo if hh == 0 else jnp.logical_not(lane_lo)
                qm = jnp.where(keep, q, 0.0).astype(BF16)
                s = _dot_nt(qm, k) + tbl_ref[0, hh]
                if j == 0:
                    s = s + first
                p, lse = _softmax_parts(s)
                outs.append(_dot(p.astype(BF16), v))
                lses.append(lse)
            o_ref[0, rows, lanes] = jnp.where(lane_lo, outs[0], outs[1])
            lse_ref[0, rows, lanes] = jnp.where(lane_lo, lses[0], lses[1])


def _attn_prompt(qa, ka, va, group, batch, seq):
    win, dil = ATT_GROUPS[group]
    n_back = win // dil
    nb = seq // dil // BAND
    qb = min(ATTN_Q_BLOCKS, nb)
    nbq = nb // qb
    nres = min(dil, ATTN_Q_BLOCKS // qb)
    slopes = _alibi_slopes()[group * ATT_HPG:(group + 1) * ATT_HPG]
    qi = jnp.arange(BAND)[:, None]
    kc = jnp.arange(2 * BAND)[None, :]
    j = qi - kc + BAND
    valid = (j >= 0) & (j <= n_back)
    tbl = jnp.where(valid[None], -slopes[:, None, None] * (dil * j).astype(F32)[None], NEG_INF)
    tbl = tbl.reshape(SLABS_PER_GROUP, 2, BAND, 2 * BAND)

    cur = lambda s, b, r, i: (s, b * nbq + i, r)
    prev = lambda s, b, r, i: (s, b * nb + jnp.maximum(i * qb - 1, 0), r)
    cur_blk = pl.BlockSpec((1, qb * BAND, nres * LANES), cur)
    prev_blk = pl.BlockSpec((1, BAND, nres * LANES), prev)
    out = jax.ShapeDtypeStruct(qa.shape, F32)
    return pl.pallas_call(
        functools.partial(_attn_prompt_kernel, qb=qb, nres=nres),
        grid=(SLABS_PER_GROUP, batch, dil // nres, nbq),
        in_specs=[
            cur_blk, prev_blk, cur_blk, prev_blk, cur_blk,
            pl.BlockSpec((1, 2, BAND, 2 * BAND), lambda s, b, r, i: (s, 0, 0, 0)),
        ],
        out_specs=[cur_blk, cur_blk],
        out_shape=[out, out],
        compiler_params=_cparams(("parallel", "parallel", "parallel", "parallel")),
        name=f"attn_prompt_g{group}",
    )(qa, ka, ka, va, va, tbl)


def _shift_cache(cache_ref, newc_ref, b, tail, t):
    ncol = cache_ref.shape[-1] // LANES
    keep_lanes = lax.broadcasted_iota(I32, (1, LANES), 1) < LANES - t
    prev_rot = None
    for c in range(ncol):
        rot = pltpu.roll(cache_ref[b, :, c * LANES:(c + 1) * LANES], LANES - t, axis=1)
        if c > 0:
            newc_ref[b, :, (c - 1) * LANES:c * LANES] = jnp.where(keep_lanes, prev_rot, rot)
        prev_rot = rot
    newc_ref[b, :, (ncol - 1) * LANES:ncol * LANES] = jnp.where(keep_lanes, prev_rot, tail)


def _attn_sample_kernel(cache_ref, q_ref, kn_ref, vn_ref, tblc_ref, tbln_ref, newc_ref, o_ref, lse_ref, *, wc, t, bb):
    kw = ATT_OUT_W
    rows = ATT_HPG * t
    rhead = lax.broadcasted_iota(I32, (rows, kw), 0) // t
    lhead = lax.broadcasted_iota(I32, (rows, kw), 1) // ATT_DH
    hm = rhead == lhead
    for b in range(bb):
        tok = slice(b * t, (b + 1) * t)
        kn = jnp.concatenate([kn_ref[0, tok], kn_ref[1, tok]], axis=1)
        vn = jnp.concatenate([vn_ref[0, tok], vn_ref[1, tok]], axis=1)
        pad = jnp.zeros((BAND - t, kw), F32)
        knp = jnp.concatenate([kn, pad], axis=0)
        vnp = jnp.concatenate([vn, pad], axis=0)

        new_t = jnp.concatenate([knp.T, vnp.T], axis=0)
        tail = pltpu.roll(new_t, LANES - t, axis=1)
        _shift_cache(cache_ref, newc_ref, b, tail, t)

        q = jnp.concatenate([q_ref[0, tok], q_ref[1, tok]], axis=1)
        qb = jnp.where(hm, jnp.concatenate([q] * ATT_HPG, axis=0), 0.0).astype(BF16)
        kt = cache_ref[b, 0:kw, :].astype(BF16)
        vt = cache_ref[b, kw:2 * kw, :].astype(BF16)
        s_c = _dot(qb, kt) + tblc_ref[...]
        s_n = _dot_nt(qb, knp.astype(BF16)) + tbln_ref[...]
        m = jnp.maximum(jnp.max(s_c, axis=-1, keepdims=True), jnp.max(s_n, axis=-1, keepdims=True))
        e_c = jnp.exp(s_c - m)
        e_n = jnp.exp(s_n - m)
        den = jnp.sum(e_c, axis=-1, keepdims=True) + jnp.sum(e_n, axis=-1, keepdims=True)
        o = _dot_nt((e_c / den).astype(BF16), vt) + _dot((e_n / den).astype(BF16), vnp.astype(BF16))
        lse = m + jnp.log(den)
        o = jnp.where(hm, o, 0.0)
        lse = jnp.where(hm, lse, 0.0)
        o_sel = o[0:t]
        lse_sel = lse[0:t]
        for h in range(1, ATT_HPG):
            o_sel = o_sel + o[h * t:(h + 1) * t]
            lse_sel = lse_sel + lse[h * t:(h + 1) * t]
        for s in range(SLABS_PER_GROUP):
            o_ref[s, tok, :] = o_sel[:, s * LANES:(s + 1) * LANES]
            lse_ref[s, tok, :] = lse_sel[:, s * LANES:(s + 1) * LANES]


def _cache_view(cache):
    batch, wc = cache.shape[:2]
    return jnp.transpose(cache, (0, 2, 3, 4, 1)).reshape(batch, 2 * ATT_OUT_W, wc)


def _cache_unview(cache_t):
    batch, _, wc = cache_t.shape
    return jnp.transpose(cache_t.reshape(batch, 2, ATT_HPG, ATT_DH, wc), (0, 4, 1, 2, 3))


def _attn_sample(qa, ka, va, cache, group, batch, t):
    win, dil = ATT_GROUPS[group]
    n_back = win // dil
    wc = cache.shape[1]
    assert wc == win and wc % LANES == 0 and t % SUBLANES == 0 and t <= BAND
    kw = ATT_OUT_W
    bb = max(1, min(ATTN_SAMPLE_BLOCK_BYTES // (2 * kw * wc * 4), ATTN_SAMPLE_MAX_BB, batch))
    assert batch % bb == 0
    slopes = _alibi_slopes()[group * ATT_HPG:(group + 1) * ATT_HPG]
    qi = jnp.arange(t)[:, None]
    idx = jnp.concatenate([jnp.arange(wc), wc + jnp.arange(BAND)])[None, :]
    dist = wc + qi - idx
    valid = (dist >= 0) & (dist % dil == 0) & (dist <= dil * n_back) & (idx < wc + t)
    tbl = jnp.where(valid[None], -slopes[:, None, None] * dist.astype(F32)[None], NEG_INF)
    tbl = tbl.reshape(ATT_HPG * t, wc + BAND)
    tblc, tbln = tbl[:, :wc], tbl[:, wc:]

    s0 = group
    new_spec = pl.BlockSpec((SLABS_PER_GROUP, bb * t, LANES), lambda b: (s0, b, 0))
    out_spec = pl.BlockSpec((SLABS_PER_GROUP, bb * t, LANES), lambda b: (0, b, 0))
    cache_spec = pl.BlockSpec((bb, 2 * kw, wc), lambda b: (b, 0, 0))
    out = jax.ShapeDtypeStruct((SLABS_PER_GROUP, batch * t, LANES), F32)
    newc, o, lse = pl.pallas_call(
        functools.partial(_attn_sample_kernel, wc=wc, t=t, bb=bb),
        grid=(batch // bb,),
        in_specs=[
            cache_spec, new_spec, new_spec, new_spec,
            pl.BlockSpec(tblc.shape, lambda b: (0, 0)),
            pl.BlockSpec(tbln.shape, lambda b: (0, 0)),
        ],
        out_specs=[cache_spec, out_spec, out_spec],
        out_shape=[jax.ShapeDtypeStruct((batch, 2 * kw, wc), F32), out, out],
        compiler_params=_cparams(("parallel",)),
        name=f"attn_sample_g{group}",
    )(_cache_view(cache), qa, ka, va, tblc, tbln)
    return o, lse, _cache_unview(newc)


def _merge_kernel(x_ref, oret_ref, gate_ref, o0_ref, o1_ref, o2_ref, l0_ref, l1_ref, l2_ref,
                  wa_ref, wb_ref, wo_ref, x1_ref, *scratch, dilated, tm):
    def rows(ref, g, s, kind):
        dil = ATT_GROUPS[g][1]
        if not dilated or dil == 1:
            return ref[s]
        scr = scratch[0]
        slot = ((g - 1) * SLABS_PER_GROUP + s) * 2 + kind
        for r in range(dil):
            scr[slot, pl.ds(r, tm // dil, stride=dil), :] = ref[s, :, r * LANES:(r + 1) * LANES]
        return scr[slot]

    slabs = []
    for s in range(SLABS_PER_GROUP):
        ls = [rows(ref, g, s, 0) for g, ref in enumerate((l0_ref, l1_ref, l2_ref))]
        os_ = [rows(ref, g, s, 1) for g, ref in enumerate((o0_ref, o1_ref, o2_ref))]
        m = jnp.maximum(jnp.maximum(ls[0], ls[1]), ls[2])
        es = [jnp.exp(l - m) for l in ls]
        den = es[0] + es[1] + es[2]
        slabs.append((es[0] / den) * os_[0] + (es[1] / den) * os_[1] + (es[2] / den) * os_[2])
    o_att = jnp.concatenate(slabs, axis=1).astype(BF16)
    ga = gate_ref[:, 0:D_MODEL].astype(F32)
    gb = gate_ref[:, D_MODEL:GATE_W].astype(F32)
    merged = _sigmoid(ga) * _dot(oret_ref[...], wa_ref[...]) + _sigmoid(gb) * _dot(o_att, wb_ref[...])
    x1_ref[...] = x_ref[...] + _dot(merged.astype(BF16), wo_ref[...])


def _merge(x, oret, zgate, os_, ls, wa, wb, wo, dilated):
    n = x.shape[0]
    tm = min(TM_MERGE, n)
    row = lambda w: pl.BlockSpec((tm, w), lambda i: (i, 0))
    full = lambda a: pl.BlockSpec(a.shape, lambda i: (0, 0))

    def slab(g):
        dil = ATT_GROUPS[g][1] if dilated else 1
        return pl.BlockSpec((SLABS_PER_GROUP, tm // dil, dil * LANES), lambda i: (0, i, 0))

    slabs = [slab(g) for g in range(len(ATT_GROUPS))]
    n_scr = (len(ATT_GROUPS) - 1) * SLABS_PER_GROUP * 2
    return pl.pallas_call(
        functools.partial(_merge_kernel, dilated=dilated, tm=tm),
        grid=(n // tm,),
        in_specs=[row(D_MODEL), row(RET_V_W), row(GATE_W)] + slabs + slabs + [full(wa), full(wb), full(wo)],
        out_specs=row(D_MODEL),
        out_shape=jax.ShapeDtypeStruct((n, D_MODEL), F32),
        scratch_shapes=[pltpu.VMEM((n_scr, tm, LANES), F32)] if dilated else [],
        compiler_params=_cparams(("parallel",)),
        name="merge_dilated" if dilated else "merge",
    )(x, oret, zgate, *os_, *ls, wa, wb, wo)


def _router_kernel(xp_ref, xs_ref, g_ref, wr_ref, br_ref, tri_ref, h_ref, idx_ref, rank_ref, w_ref, cnt_ref,
                   run_scr, *, n_prompt_tiles, tm):
    i = pl.program_id(0)

    @pl.when(i == 0)
    def _():
        run_scr[...] = jnp.zeros_like(run_scr)

    x = jnp.where(i < n_prompt_tiles, xp_ref[...], xs_ref[...])
    h = _rms(x, g_ref[...])
    for j in range(ROW_TILES):
        h_ref[pl.ds(j, tm, stride=ROW_TILES), :] = h[:, j * LANES:(j + 1) * LANES]

    logits = jnp.dot(h, wr_ref[...], precision=lax.Precision.HIGHEST, preferred_element_type=F32) + br_ref[...]
    lanes = lax.broadcasted_iota(I32, (1, LANES), 1)
    lanes_f = lanes.astype(F32)
    work = logits
    vals, idxs, hots = [], [], []
    for _ in range(TOP_K):
        m = jnp.max(work, axis=-1, keepdims=True)
        idx = jnp.min(jnp.where(work == m, lanes_f, float(LANES)), axis=-1, keepdims=True)
        hot = lanes_f == idx
        vals.append(m)
        idxs.append(idx)
        hots.append(hot)
        work = jnp.where(hot, -jnp.inf, work)
    exps = [jnp.exp(v - vals[0]) for v in vals]
    den = exps[0] + exps[1] + exps[2] + exps[3]

    cat = jnp.concatenate([hot.astype(BF16) for hot in hots], axis=1)
    cum = _dot(tri_ref[...], cat)
    prev = run_scr[...]
    idx_out = jnp.zeros((tm, LANES), F32)
    rank_out = jnp.zeros((tm, LANES), F32)
    w_out = jnp.zeros((tm, LANES), F32)
    for k in range(TOP_K):
        hot_f = hots[k].astype(F32)
        rank = jnp.sum(hot_f * (cum[:, k * LANES:(k + 1) * LANES] + prev), axis=-1, keepdims=True)
        prev = prev + jnp.sum(hot_f, axis=0, keepdims=True)
        idx_out = jnp.where(lanes == k, idxs[k], idx_out)
        rank_out = jnp.where(lanes == k, rank, rank_out)
        w_out = jnp.where(lanes == k, exps[k] / den, w_out)
    run_scr[...] = prev
    idx_ref[...] = idx_out.T[:SUBLANES].astype(I32)
    rank_ref[...] = rank_out.T[:SUBLANES].astype(I32)
    w_ref[...] = w_out
    cnt_ref[...] = prev.astype(I32)


def _router(x1p, x1s, g_moe, wr_pad, br_pad):
    tm = TM_ROUTE
    npt, nst = x1p.shape[0] // tm, x1s.shape[0] // tm
    n = x1p.shape[0] + x1s.shape[0]
    tri = (jnp.arange(tm)[:, None] > jnp.arange(tm)[None, :]).astype(BF16)
    row128 = pl.BlockSpec((tm, LANES), lambda i: (i, 0))
    choice = pl.BlockSpec((SUBLANES, tm), lambda i: (0, i))
    return pl.pallas_call(
        functools.partial(_router_kernel, n_prompt_tiles=npt, tm=tm),
        grid=(npt + nst,),
        in_specs=[
            pl.BlockSpec((tm, D_MODEL), lambda i: (jnp.minimum(i, npt - 1), 0)),
            pl.BlockSpec((tm, D_MODEL), lambda i: (jnp.clip(i - npt, 0, nst - 1), 0)),
            pl.BlockSpec((1, D_MODEL), lambda i: (0, 0)),
            pl.BlockSpec((D_MODEL, LANES), lambda i: (0, 0)),
            pl.BlockSpec((1, LANES), lambda i: (0, 0)),
            pl.BlockSpec((tm, tm), lambda i: (0, 0)),
        ],
        out_specs=[
            pl.BlockSpec((tm * ROW_TILES, LANES), lambda i: (i, 0)),
            choice, choice, row128,
            pl.BlockSpec((1, LANES), lambda i: (0, 0)),
        ],
        out_shape=[
            jax.ShapeDtypeStruct((n * ROW_TILES, LANES), F32),
            jax.ShapeDtypeStruct((SUBLANES, n), I32),
            jax.ShapeDtypeStruct((SUBLANES, n), I32),
            jax.ShapeDtypeStruct((n, LANES), F32),
            jax.ShapeDtypeStruct((1, LANES), I32),
        ],
        scratch_shapes=[pltpu.VMEM((1, LANES), F32)],
        compiler_params=_cparams(("arbitrary",)),
        name="router",
    )(x1p, x1s, g_moe, wr_pad, br_pad, tri)


def _row_copy(src_ref, src_row, dst_ref, dst_row, sem):
    return pltpu.make_async_copy(
        src_ref.at[pl.ds(pl.multiple_of(src_row * ROW_TILES, ROW_TILES), ROW_TILES), :],
        dst_ref.at[pl.ds(pl.multiple_of(dst_row * ROW_TILES, ROW_TILES), ROW_TILES), :],
        sem)


def _dispatch_kernel(zstart_ref, zlen_ref, nused_ref, slots_ref, h_ref, hs_ref, zero_scr, sem, *, tm, n_tiles):
    i = pl.program_id(0)

    def zero_copy(row, n_rows):
        start = pl.multiple_of(row * ROW_TILES, ROW_TILES)
        return pltpu.make_async_copy(zero_scr.at[pl.ds(0, n_rows * ROW_TILES), :],
                                     hs_ref.at[pl.ds(start, n_rows * ROW_TILES), :], sem)

    def pad_copies(e, op):
        row, left = zstart_ref[e], zlen_ref[e]
        for b in reversed(range(TM_MOE.bit_length() - 1)):
            bit = (left >> b) & 1

            @pl.when(bit == 1)
            def _():
                op(zero_copy(row, 1 << b))

            row = row + (bit << b)

    def all_copies(op):
        lax.fori_loop(0, N_EXPERTS, lambda e, c: (pad_copies(e, op), c)[1], 0)
        lax.fori_loop(nused_ref[0], n_tiles, lambda tt, c: (op(zero_copy(tt * TM_MOE, TM_MOE)), c)[1], 0)

    @pl.when(i == 0)
    def _():
        zero_scr[...] = jnp.zeros_like(zero_scr)
        all_copies(lambda c: c.start())
        all_copies(lambda c: c.wait())

    def body(tok, carry):
        for k in range(TOP_K):
            _row_copy(h_ref, tok, hs_ref, slots_ref[0, 0, k * tm + tok], sem).start(priority=k % 2)
        return carry

    lax.fori_loop(0, tm, body, 0, unroll=8)
    for _ in range(TOP_K):
        pltpu.make_async_copy(h_ref, hs_ref.at[pl.ds(0, tm * ROW_TILES), :], sem).wait()


def _dispatch(hrow, slots, zstart, zlen, n_used, n_tiles):
    tm = TM_ROUTE
    n = hrow.shape[0] // ROW_TILES
    nt = n // tm
    slots3 = jnp.transpose(slots.reshape(TOP_K, nt, tm), (1, 0, 2)).reshape(nt, 1, TOP_K * tm)
    grid_spec = pltpu.PrefetchScalarGridSpec(
        num_scalar_prefetch=3,
        grid=(nt,),
        in_specs=[
            pl.BlockSpec((1, 1, tm * TOP_K), lambda i, z, l, u: (i, 0, 0), memory_space=pltpu.SMEM),
            pl.BlockSpec((tm * ROW_TILES, LANES), lambda i, z, l, u: (i, 0)),
        ],
        out_specs=pl.BlockSpec(memory_space=pl.ANY),
        scratch_shapes=[pltpu.VMEM((TM_MOE * ROW_TILES, LANES), F32), pltpu.SemaphoreType.DMA(())],
    )
    return pl.pallas_call(
        functools.partial(_dispatch_kernel, tm=tm, n_tiles=n_tiles),
        grid_spec=grid_spec,
        out_shape=jax.ShapeDtypeStruct((n_tiles * TM_MOE * ROW_TILES, LANES), F32),
        compiler_params=_cparams(("arbitrary",)),
        name="dispatch",
    )(zstart, zlen, n_used, slots3, hrow)


def _experts_kernel(texp_ref, nused_ref, hs_ref, wu_ref, bu_ref, wd_ref, bd_ref, ys_ref, wu_scr, wd_scr):
    j = pl.program_id(0)
    tm = TM_MOE
    n_used = nused_ref[0]
    jj = jnp.minimum(j, n_used - 1)
    new_expert = (j == 0) | (texp_ref[jj] != texp_ref[jnp.maximum(jj - 1, 0)])

    @pl.when((j < n_used) & new_expert)
    def _():
        wu_scr[...] = wu_ref[0].astype(BF16)
        wd_scr[...] = wd_ref[0].astype(BF16)

    @pl.when(j < n_used)
    def _():
        x = jnp.concatenate(
            [hs_ref[pl.ds(c, tm, stride=ROW_TILES), :] for c in range(ROW_TILES)], axis=1).astype(BF16)
        u = _dot(x, wu_scr[...]) + bu_ref[0]
        glu = jnp.minimum(u[:, :D_EXPERT], SWIGLU_LIMIT)
        lin = jnp.clip(u[:, D_EXPERT:], -SWIGLU_LIMIT, SWIGLU_LIMIT)
        a = glu * _sigmoid(SWIGLU_ALPHA * glu) * (lin + 1.0)
        y = _dot(a.astype(BF16), wd_scr[...]) + bd_ref[0]
        for c in range(ROW_TILES):
            ys_ref[pl.ds(c, tm, stride=ROW_TILES), :] = y[:, c * LANES:(c + 1) * LANES]

    @pl.when(j >= n_used)
    def _():
        ys_ref[...] = jnp.zeros_like(ys_ref)


def _experts(hs, tile_expert, n_used, w_up, b_up, w_down, b_down, n_tiles):
    tm = TM_MOE
    tile = lambda j, te, nu: (jnp.minimum(j, nu[0] - 1), 0)
    exp3 = lambda j, te, nu: (te[jnp.minimum(j, nu[0] - 1)], 0, 0)
    grid_spec = pltpu.PrefetchScalarGridSpec(
        num_scalar_prefetch=2,
        grid=(n_tiles,),
        in_specs=[
            pl.BlockSpec((tm * ROW_TILES, LANES), tile),
            pl.BlockSpec((1, D_MODEL, 2 * D_EXPERT), exp3),
            pl.BlockSpec((1, 1, 2 * D_EXPERT), exp3),
            pl.BlockSpec((1, D_EXPERT, D_MODEL), exp3),
            pl.BlockSpec((1, 1, D_MODEL), exp3),
        ],
        out_specs=pl.BlockSpec((tm * ROW_TILES, LANES), lambda j, te, nu: (j, 0)),
        scratch_shapes=[pltpu.VMEM((D_MODEL, 2 * D_EXPERT), BF16), pltpu.VMEM((D_EXPERT, D_MODEL), BF16)],
    )
    return pl.pallas_call(
        _experts_kernel,
        grid_spec=grid_spec,
        out_shape=jax.ShapeDtypeStruct((n_tiles * tm * ROW_TILES, LANES), F32),
        compiler_params=_cparams(("arbitrary",)),
        name="experts",
    )(tile_expert, n_used, hs, w_up, b_up.reshape(N_EXPERTS, 1, -1), w_down, b_down.reshape(N_EXPERTS, 1, -1))


def _combine_kernel(slots_ref, nslots_ref, x1_ref, w_ref, p_ref, g_ref, wpg_ref, wpe_ref, ys_ref, y_ref,
                    buf, x2_scr, sems, *, tm):
    i = pl.program_id(0)
    cur = lax.rem(i, 2)
    nxt = 1 - cur

    def gather(sref, half, tok, k):
        return _row_copy(ys_ref, sref[0, 0, k * tm + tok], buf.at[half, k], tok, sems.at[half])

    def wait_tile(half):
        for k in range(TOP_K):
            pltpu.make_async_copy(ys_ref.at[pl.ds(0, tm * ROW_TILES), :], buf.at[half, k], sems.at[half]).wait()

    @pl.when(i == 0)
    def _():
        def body(tok, carry):
            for k in range(TOP_K):
                gather(slots_ref, 0, tok, k).start(priority=k % 2)
            return carry

        lax.fori_loop(0, tm, body, 0)

    wait_tile(cur)
    rg = COMB_ROW_GROUP
    per = tm // (tm // rg * ROW_TILES)
    for r in range(tm // rg):
        w = w_ref[r * rg:(r + 1) * rg, :]
        wk = [jnp.broadcast_to(w[:, k:k + 1], (rg, LANES)) for k in range(TOP_K)]
        for c in range(ROW_TILES):
            acc = x1_ref[r * rg:(r + 1) * rg, c * LANES:(c + 1) * LANES]
            for k in range(TOP_K):
                acc = acc + wk[k] * buf[cur, k, pl.ds(r * rg * ROW_TILES + c, rg, stride=ROW_TILES), :]
            g = r * ROW_TILES + c
            for tok in range(g * per, (g + 1) * per):
                for k in range(TOP_K):
                    gather(nslots_ref, nxt, tok, k).start(priority=k % 2)
            x2_scr[r * rg:(r + 1) * rg, c * LANES:(c + 1) * LANES] = acc
    x2 = x2_scr[...]
    gate = _sigmoid(_dot(_rms(x2, g_ref[...]).astype(BF16), wpg_ref[...]))
    y_ref[...] = x2 + gate * _dot(p_ref[...].astype(BF16), wpe_ref[...])

    @pl.when(i == pl.num_programs(0) - 1)
    def _():
        wait_tile(nxt)


def _combine(x1, slots, topw, p, g_ple, wpg, wpe, ys):
    n = x1.shape[0]
    tm = TM_COMB
    nt = n // tm
    slots3 = jnp.transpose(slots.reshape(TOP_K, nt, tm), (1, 0, 2)).reshape(nt, 1, TOP_K * tm)
    full = lambda a: pl.BlockSpec(a.shape, lambda i: (0, 0))
    return pl.pallas_call(
        functools.partial(_combine_kernel, tm=tm),
        grid=(nt,),
        in_specs=[
            pl.BlockSpec((1, 1, tm * TOP_K), lambda i: (i, 0, 0), memory_space=pltpu.SMEM),
            pl.BlockSpec((1, 1, tm * TOP_K), lambda i: (jnp.minimum(i + 1, nt - 1), 0, 0), memory_space=pltpu.SMEM),
            pl.BlockSpec((tm, D_MODEL), lambda i: (i, 0)),
            pl.BlockSpec((tm, LANES), lambda i: (i, 0)),
            pl.BlockSpec((tm, PLE_DIM), lambda i: (i, 0)),
            full(g_ple), full(wpg), full(wpe),
            pl.BlockSpec(memory_space=pl.ANY),
        ],
        out_specs=pl.BlockSpec((tm, D_MODEL), lambda i: (i, 0)),
        out_shape=jax.ShapeDtypeStruct((n, D_MODEL), F32),
        scratch_shapes=[pltpu.VMEM((2, TOP_K, tm * ROW_TILES, LANES), F32), pltpu.VMEM((tm, D_MODEL), F32),
                        pltpu.SemaphoreType.DMA((2,))],
        compiler_params=_cparams(("arbitrary",)),
        name="combine",
    )(slots3, slots3, x1, topw, p, g_ple, wpg, wpe, ys)


def _prompt_cache_kernel(k_ref, v_ref, o_ref, *scratch, dil, keep):
    for kv, ref in enumerate((k_ref, v_ref)):
        for s in range(SLABS_PER_GROUP):
            if dil == 1:
                tok = ref[s]
            else:
                scr = scratch[0]
                for r in range(dil):
                    scr[pl.ds(r, keep // dil, stride=dil), :] = ref[s, :, r * LANES:(r + 1) * LANES]
                tok = scr[...]
            row0 = kv * ATT_OUT_W + s * LANES
            o_ref[0, row0:row0 + LANES, :] = tok.T


def _cache_from_slabs(ka, va, group, batch, seq, keep):
    dil = ATT_GROUPS[group][1]
    assert keep % dil == 0 and seq % keep == 0 and keep % LANES == 0
    blocks_per_seq = seq // keep
    tail = pl.BlockSpec((SLABS_PER_GROUP, keep // dil, dil * LANES), lambda b: (0, (b + 1) * blocks_per_seq - 1, 0))
    cache_t = pl.pallas_call(
        functools.partial(_prompt_cache_kernel, dil=dil, keep=keep),
        grid=(batch,),
        in_specs=[tail, tail],
        out_specs=pl.BlockSpec((1, 2 * ATT_OUT_W, keep), lambda b: (b, 0, 0)),
        out_shape=jax.ShapeDtypeStruct((batch, 2 * ATT_OUT_W, keep), F32),
        scratch_shapes=[pltpu.VMEM((keep, LANES), F32)] if dil > 1 else [],
        compiler_params=_cparams(("parallel",)),
        name=f"prompt_cache_g{group}",
    )(ka, va)
    return _cache_unview(cache_t)


def _layer(xp, xs, pp, ps, state, caches, g_mix, w_in, ret_gn, qn_g, kn_g, w_a, w_b, w_o, g_moe, w_router, b_router,
           w_up, b_up, w_down, b_down, g_ple, w_ple_gate, w_ple):
    batch, seq, _ = xp.shape
    dbatch, dseq, _ = xs.shape
    n_p, n_s = batch * seq, dbatch * dseq
    n = n_p + n_s

    w_in_bf = w_in.astype(BF16)
    g_mix2 = g_mix.reshape(1, D_MODEL)
    qg = qn_g.reshape(N_SLABS, 1, LANES)
    kg = kn_g.reshape(N_SLABS, 1, LANES)
    gn = ret_gn.reshape(RET_HEADS, 1, RET_DV)
    wa, wb, wo = w_a.astype(BF16), w_b.astype(BF16), w_o.astype(BF16)

    xp2 = xp.reshape(n_p, D_MODEL)
    xs2 = xs.reshape(n_s, D_MODEL)
    zret_p, zgate_p, *qkv_p = _inproj(xp2, g_mix2, w_in_bf, qg, kg, dilated=True)
    zret_s, zgate_s, qa_s, ka_s, va_s = _inproj(xs2, g_mix2, w_in_bf, qg, kg, dilated=False)

    oret_p, state_p = _ret_prompt(zret_p, gn, batch, seq)
    oret_s, state_s = _ret_sample(zret_s, state, gn, dbatch, dseq)

    o_p, l_p, o_s, l_s, win_p, win_s = [], [], [], [], [], []
    for g, (win, _) in enumerate(ATT_GROUPS):
        qa_g, ka_g, va_g = qkv_p[3 * g:3 * g + 3]
        o, l = _attn_prompt(qa_g, ka_g, va_g, g, batch, seq)
        o_p.append(o)
        l_p.append(l)
        win_p.append(_cache_from_slabs(ka_g, va_g, g, batch, seq, min(win, seq)))
        o, l, newc = _attn_sample(qa_s, ka_s, va_s, caches[g], g, dbatch, dseq)
        o_s.append(o)
        l_s.append(l)
        win_s.append(newc)

    x1p = _merge(xp2, oret_p, zgate_p, o_p, l_p, wa, wb, wo, dilated=True)
    x1s = _merge(xs2, oret_s, zgate_s, o_s, l_s, wa, wb, wo, dilated=False)

    wr_pad = jnp.zeros((D_MODEL, LANES), F32).at[:, :N_EXPERTS].set(w_router)
    br_pad = jnp.full((1, LANES), NEG_INF, F32).at[0, :N_EXPERTS].set(b_router)
    hrow, top_i, rank, top_w, counts = _router(x1p, x1s, g_moe.reshape(1, D_MODEL), wr_pad, br_pad)

    cnt = counts[0, :N_EXPERTS]
    padded = (cnt + TM_MOE - 1) // TM_MOE * TM_MOE
    ends = jnp.cumsum(padded)
    goff = ends - padded
    slots = rank[:TOP_K]
    for e in range(N_EXPERTS):
        slots = slots + jnp.where(top_i[:TOP_K] == e, goff[e], 0)
    n_tiles = (n * TOP_K) // TM_MOE + N_EXPERTS
    tile_ends = ends // TM_MOE
    n_used = tile_ends[-1:].astype(I32)
    tile_expert = jnp.sum(tile_ends[None, :] <= jnp.arange(n_tiles)[:, None], axis=1)
    tile_expert = jnp.minimum(tile_expert, N_EXPERTS - 1).astype(I32)
    zstart = (goff + cnt).astype(I32)
    zlen = (padded - cnt).astype(I32)

    hs = _dispatch(hrow, slots, zstart, zlen, n_used, n_tiles)
    ys = _experts(hs, tile_expert, n_used, w_up, b_up, w_down, b_down, n_tiles)

    g_ple2 = g_ple.reshape(1, D_MODEL)
    wpg, wpe = w_ple_gate.astype(BF16), w_ple.astype(BF16)
    yp = _combine(x1p, slots[:, :n_p], top_w[:n_p], pp.reshape(n_p, PLE_DIM), g_ple2, wpg, wpe, ys)
    ys_out = _combine(x1s, slots[:, n_p:], top_w[n_p:], ps.reshape(n_s, PLE_DIM), g_ple2, wpg, wpe, ys)
    return (yp.reshape(batch, seq, D_MODEL), ys_out.reshape(dbatch, dseq, D_MODEL), state_p, state_s, win_p, win_s)


def kernel(x_prompt, x_sample, p_prompt, p_sample, state_ret, cache_win128_kv, cache_win512_kv, cache_win2048_kv, norm_mix_g, w_in, ret_norm_g, q_norm_g, k_norm_g, w_a, w_b, w_o, norm_moe_g, w_router, b_router, w_up, b_up, w_down, b_down, norm_ple_g, w_ple_gate, w_ple):
    caches = (cache_win128_kv, cache_win512_kv, cache_win2048_kv)
    depth = w_in.shape[0]
    xp, xs = x_prompt, x_sample
    ret_p, ret_s = [], []
    win_p = [[] for _ in ATT_GROUPS]
    win_s = [[] for _ in ATT_GROUPS]
    for i in range(depth):
        xp, xs, sp, ss, bp, bs = _layer(
            xp, xs, p_prompt[i], p_sample[i], state_ret[i], [c[i] for c in caches], norm_mix_g[i], w_in[i],
            ret_norm_g[i], q_norm_g[i], k_norm_g[i], w_a[i], w_b[i], w_o[i], norm_moe_g[i], w_router[i], b_router[i],
            w_up[i], b_up[i], w_down[i], b_down[i], norm_ple_g[i], w_ple_gate[i], w_ple[i])
        ret_p.append(sp)
        ret_s.append(ss)
        for g in range(len(ATT_GROUPS)):
            win_p[g].append(bp[g])
            win_s[g].append(bs[g])
    return (xp, xs, jnp.stack(ret_p), jnp.stack(ret_s), jnp.stack(win_p[0]), jnp.stack(win_s[0]),
            jnp.stack(win_p[1]), jnp.stack(win_s[1]), jnp.stack(win_p[2]), jnp.stack(win_s[2]))
```
